```python
import jax, jax.numpy as jnp
from jax import lax
import numpy as np

D_MODEL = 1024
BATCH = 2
SEQ = 16384
DEPTH = 2

CHUNK = 64

SB_HEAD_DIM = 64
SB_HEADS = 4
SB_WIDTH = SB_HEADS * SB_HEAD_DIM
Q_BLOCK = 128
SUB = 32

POOL_WINDOWS = (2, 4, 8, 16)
POOL_GROUPS = len(POOL_WINDOWS)
POOL_WIDTH = D_MODEL - SB_WIDTH
POOL_GROUP_DIM = POOL_WIDTH // POOL_GROUPS

N_BRANCHES = 2
IN_WIDTH = 3 * SB_WIDTH + POOL_WIDTH + N_BRANCHES * D_MODEL

N_EXPERTS = 32
TOP_K = 4
D_EXPERT = D_MODEL
SWIGLU_LIMIT = 7.0
SWIGLU_ALPHA = 1.702
EXPERT_BLOCK = 256

NORM_EPS = 1e-6

kernel_name = "hybrid_stickbreak_pool_moe_encoder"


def rmsnorm(x, g):
    xf = x.astype(jnp.float32)
    r = lax.rsqrt(jnp.mean(xf * xf, axis=-1, keepdims=True) + NORM_EPS)
    return (xf * r * g.astype(jnp.float32)).astype(x.dtype)


def stick_breaking_attention(q, k, v):
    B, S, H, hd = q.shape
    qf = q.astype(jnp.float32) * (1.0 / np.sqrt(hd).astype(np.float32))
    kf = k.astype(jnp.float32)
    vf = v.astype(jnp.float32)
    tri_incl = jnp.tri(SUB, dtype=jnp.float32)
    outs = []
    for i in range(S // Q_BLOCK):
        n_k = (i + 1) * Q_BLOCK
        n_sub = n_k // SUB
        qb = qf[:, i * Q_BLOCK:(i + 1) * Q_BLOCK]
        kb = kf[:, :n_k]
        vb = vf[:, :n_k]
        z = jnp.einsum('bqhd,bkhd->bhqk', qb, kb)
        qpos = i * Q_BLOCK + jnp.arange(Q_BLOCK)
        mask = jnp.arange(n_k)[None, :] < qpos[:, None]
        log_keep = jnp.where(mask, jax.nn.log_sigmoid(-z), 0.0)
        lr = log_keep.reshape(B, H, Q_BLOCK, n_sub, SUB)
        r_in = jnp.einsum('bhqnj,jk->bhqnk', lr, tri_incl)
        after = jnp.einsum('bhqn,nm->bhqm', lr.sum(-1), jnp.tri(n_sub, k=-1, dtype=jnp.float32))
        log_a = z + (r_in + after[..., None]).reshape(B, H, Q_BLOCK, n_k)
        a = jnp.exp(jnp.where(mask, log_a, -jnp.inf))
        outs.append(jnp.einsum('bhqk,bkhd->bqhd', a, vb))
    o = jnp.concatenate(outs, axis=1)
    return o.reshape(B, S, H * hd)


def multi_scale_pool(u, pool_w, pool_b, pool_scale):
    B, S, G, Cg = u.shape
    uf = u.astype(jnp.float32)
    cs0 = jnp.concatenate([jnp.zeros((B, 1, G, Cg), jnp.float32), jnp.cumsum(uf, axis=1)], axis=1)
    t1 = jnp.arange(1, S + 1).astype(jnp.float32)
    outs = []
    for g, w in enumerate(POOL_WINDOWS):
        cg = cs0[:, :, g]
        lo = jnp.pad(cg[:, :S - w + 1], ((0, 0), (w - 1, 0), (0, 0)))
        mean = (cg[:, 1:] - lo) / jnp.minimum(t1, float(w))[None, :, None]
        outs.append(mean - uf[:, :, g])
    mixed = jnp.stack(outs, axis=2)
    y = jnp.einsum('bsgc,gcd->bsgd', mixed, pool_w.astype(jnp.float32)) + pool_b.astype(jnp.float32)
    y = y * pool_scale.astype(jnp.float32)
    return y.reshape(B, S, G * Cg)


def hybrid_mixer(h, w_in, pool_w, pool_b, pool_scale, w_br_attn, w_br_pool, w_out):
    B, S, _ = h.shape
    p = h @ w_in
    q, k, v, u, g = jnp.split(p, [SB_WIDTH, 2 * SB_WIDTH, 3 * SB_WIDTH, 3 * SB_WIDTH + POOL_WIDTH], axis=-1)
    shp = (B, S, SB_HEADS, SB_HEAD_DIM)
    attn = stick_breaking_attention(q.reshape(shp), k.reshape(shp), v.reshape(shp)).astype(h.dtype)
    pool = multi_scale_pool(u.reshape(B, S, POOL_GROUPS, POOL_GROUP_DIM), pool_w, pool_b, pool_scale).astype(h.dtype)
    g_attn, g_pool = jnp.split(jax.nn.sigmoid(g), N_BRANCHES, axis=-1)
    merged = g_attn * (attn @ w_br_attn) + g_pool * (pool @ w_br_pool)
    return merged @ w_out


def moe(h, router_w, router_b, w_gu, b_gu, w_dn, b_dn):
    B, S, D = h.shape
    T = B * S
    xt = h.reshape(T, D)
    logits = (xt @ router_w + router_b).astype(jnp.float32)
    top_v, top_i = lax.top_k(logits, TOP_K)
    gates = jax.nn.softmax(top_v, axis=-1)
    e_flat = top_i.reshape(-1)
    g_flat = gates.reshape(-1)
    tok_flat = jnp.repeat(jnp.arange(T, dtype=jnp.int32), TOP_K)
    order = jnp.argsort(e_flat)
    e_sorted = e_flat[order]
    counts = jnp.bincount(e_flat, length=N_EXPERTS)
    padded = ((counts + EXPERT_BLOCK - 1) // EXPERT_BLOCK) * EXPERT_BLOCK
    start = jnp.cumsum(counts) - counts
    pend = jnp.cumsum(padded)
    pstart = pend - padded
    rank = jnp.arange(T * TOP_K) - start[e_sorted]
    dest = pstart[e_sorted] + rank
    n_rows = T * TOP_K + N_EXPERTS * EXPERT_BLOCK
    n_blocks = n_rows // EXPERT_BLOCK
    row_tok = jnp.zeros((n_rows,), jnp.int32).at[dest].set(tok_flat[order])
    row_gate = jnp.zeros((n_rows,), jnp.float32).at[dest].set(g_flat[order])
    block_e = jnp.minimum(jnp.searchsorted(pend, jnp.arange(n_blocks) * EXPERT_BLOCK, side='right'), N_EXPERTS - 1)
    xr = xt[row_tok].reshape(n_blocks, EXPERT_BLOCK, D)

    def expert_block(args):
        xb, e = args
        gu = xb @ w_gu[e] + b_gu[e]
        gate, up = jnp.split(gu, 2, axis=-1)
        gate = jnp.minimum(gate, SWIGLU_LIMIT)
        up = jnp.clip(up, -SWIGLU_LIMIT, SWIGLU_LIMIT)
        act = (up + 1.0) * gate * jax.nn.sigmoid(SWIGLU_ALPHA * gate)
        return act @ w_dn[e] + b_dn[e]

    yr = lax.map(expert_block, (xr, block_e)).reshape(n_rows, D)
    y = jax.ops.segment_sum(yr.astype(jnp.float32) * row_gate[:, None], row_tok, num_segments=T)
    return y.reshape(B, S, D).astype(h.dtype)


def setup_inputs(seed: int = 0) -> dict:
    key = jax.random.key(seed)
    ks = jax.random.split(key, 24)
    L, D = DEPTH, D_MODEL

    def nrm(k, shape, scale):
        return jax.random.normal(k, shape, jnp.float32) * scale

    return {
        "x": nrm(ks[0], (BATCH, SEQ, D), 1.0),
        "c": nrm(ks[1], (BATCH, D), 1.0),
        "ada_w": nrm(ks[2], (L, D, 6 * D), 0.5 * D ** -0.5),
        "ada_b": nrm(ks[3], (L, 6 * D), 0.01),
        "pre1_g": 1.0 + nrm(ks[4], (L, D), 0.05),
        "post1_g": 1.0 + nrm(ks[5], (L, D), 0.05),
        "pre2_g": 1.0 + nrm(ks[6], (L, D), 0.05),
        "post2_g": 1.0 + nrm(ks[7], (L, D), 0.05),
        "w_in": nrm(ks[8], (L, D, IN_WIDTH), D ** -0.5),
        "pool_w": nrm(ks[9], (L, POOL_GROUPS, POOL_GROUP_DIM, POOL_GROUP_DIM), POOL_GROUP_DIM ** -0.5),
        "pool_b": nrm(ks[10], (L, POOL_GROUPS, POOL_GROUP_DIM), 0.01),
        "pool_scale": 1.0 + nrm(ks[11], (L, POOL_GROUPS, POOL_GROUP_DIM), 0.1),
        "w_br_attn": nrm(ks[12], (L, SB_WIDTH, D), SB_WIDTH ** -0.5),
        "w_br_pool": nrm(ks[13], (L, POOL_WIDTH, D), POOL_WIDTH ** -0.5),
        "w_out": nrm(ks[14], (L, D, D), D ** -0.5),
        "router_w": nrm(ks[15], (L, D, N_EXPERTS), D ** -0.5),
        "router_b": nrm(ks[16], (L, N_EXPERTS), 0.01),
        "w_gu": nrm(ks[17], (L, N_EXPERTS, D, 2 * D_EXPERT), D ** -0.5),
        "b_gu": nrm(ks[18], (L, N_EXPERTS, 2 * D_EXPERT), 0.01),
        "w_dn": nrm(ks[19], (L, N_EXPERTS, D_EXPERT, D), D_EXPERT ** -0.5),
        "b_dn": nrm(ks[20], (L, N_EXPERTS, D), 0.01),
    }


def reference(x, c, ada_w, ada_b, pre1_g, post1_g, pre2_g, post2_g, w_in, pool_w, pool_b, pool_scale,
              w_br_attn, w_br_pool, w_out, router_w, router_b, w_gu, b_gu, w_dn, b_dn):
    c_act = jax.nn.silu(c)
    for l in range(DEPTH):
        mod = (c_act @ ada_w[l] + ada_b[l])[:, None, :]
        shift1, scale1, gate1, shift2, scale2, gate2 = jnp.split(mod, 6, axis=-1)
        h = rmsnorm(x, pre1_g[l]) * (1.0 + scale1) + shift1
        y = hybrid_mixer(h, w_in[l], pool_w[l], pool_b[l], pool_scale[l], w_br_attn[l], w_br_pool[l], w_out[l])
        x = x + gate1 * rmsnorm(y, post1_g[l])
        h = rmsnorm(x, pre2_g[l]) * (1.0 + scale2) + shift2
        y = moe(h, router_w[l], router_b[l], w_gu[l], b_gu[l], w_dn[l], b_dn[l])
        x = x + gate2 * rmsnorm(y, post2_g[l])
    return x
```

```python
import functools

import jax
import jax.numpy as jnp
from jax import lax
from jax.experimental import pallas as pl
from jax.experimental.pallas import tpu as pltpu

F32 = jnp.float32
BF16 = jnp.bfloat16

SB_HEADS = 4
SB_HEAD_DIM = 64
SB_WIDTH = SB_HEADS * SB_HEAD_DIM
POOL_WINDOWS = (2, 4, 8, 16)
POOL_GROUPS = len(POOL_WINDOWS)
POOL_HALO = 16
N_EXPERTS = 32
TOP_K = 4
ROW_BLOCK = 256
SWIGLU_LIMIT = 7.0
SWIGLU_ALPHA = 1.702
NORM_EPS = 1e-6

LANES = 128
VMEM_LIMIT = 52 * 1024 * 1024

META_E, META_G, META_R = 0, 4, 8


def _rms(x):
    return lax.rsqrt(jnp.mean(x * x, axis=-1, keepdims=True) + NORM_EPS)


def _ada_kernel(c_ref, w_ref, b_ref, o_ref):
    c = c_ref[...]
    ca = c * jax.nn.sigmoid(c)
    o_ref[0] = jnp.dot(ca, w_ref[0], precision=lax.Precision.HIGHEST,
                       preferred_element_type=F32) + b_ref[0]


def _ada(c_pad, ada_w, ada_b):
    L, D, N = ada_w.shape
    bp = c_pad.shape[0]
    tn = 1536
    return pl.pallas_call(
        _ada_kernel,
        grid=(L, N // tn),
        in_specs=[
            pl.BlockSpec((bp, D), lambda l, j: (0, 0)),
            pl.BlockSpec((1, D, tn), lambda l, j: (l, 0, j)),
            pl.BlockSpec((1, 1, tn), lambda l, j: (l, 0, j)),
        ],
        out_specs=pl.BlockSpec((1, bp, tn), lambda l, j: (l, 0, j)),
        out_shape=jax.ShapeDtypeStruct((L, bp, N), F32),
        compiler_params=pltpu.CompilerParams(
            dimension_semantics=("arbitrary", "arbitrary"), vmem_limit_bytes=VMEM_LIMIT),
        name="ada_mod",
    )(c_pad, ada_w, ada_b.reshape(L, 1, N))


def _in_proj_kernel(x_ref, g_ref, sc_ref, sh_ref, w_ref, qkv_ref, u_ref, gate_ref, *, d_pool):
    x = x_ref[...]
    h = (x * _rms(x) * g_ref[...]) * (1.0 + sc_ref[0]) + sh_ref[0]
    hb = h.astype(BF16)
    n_qkv = 3 * SB_WIDTH
    qkv = jnp.dot(hb, w_ref[:, 0:n_qkv], preferred_element_type=F32)
    col = lax.broadcasted_iota(jnp.int32, (1, n_qkv), 1)
    qkv = jnp.where(col < SB_WIDTH, qkv * (SB_HEAD_DIM ** -0.5), qkv)
    qkv_ref[...] = qkv.astype(BF16)
    u_ref[...] = jnp.dot(hb, w_ref[:, n_qkv:n_qkv + d_pool], preferred_element_type=F32)
    gl = jnp.dot(hb, w_ref[:, n_qkv + d_pool:], preferred_element_type=F32)
    gate_ref[...] = jax.nn.sigmoid(gl).astype(BF16)


def _in_proj(x2d, g, scale, shift, w_bf, *, seq, tm=512):
    T, D = x2d.shape
    n_in = w_bf.shape[1]
    n_qkv = 3 * SB_WIDTH
    d_pool = D - SB_WIDTH
    n_gate = n_in - n_qkv - d_pool
    tps = seq // tm
    vec = pl.BlockSpec((1, 1, D), lambda i: (i // tps, 0, 0))
    return pl.pallas_call(
        functools.partial(_in_proj_kernel, d_pool=d_pool),
        grid=(T // tm,),
        in_specs=[
            pl.BlockSpec((tm, D), lambda i: (i, 0)),
            pl.BlockSpec((1, D), lambda i: (0, 0)),
            vec, vec,
            pl.BlockSpec((D, n_in), lambda i: (0, 0)),
        ],
        out_specs=[
            pl.BlockSpec((tm, n_qkv), lambda i: (i, 0)),
            pl.BlockSpec((tm, d_pool), lambda i: (i, 0)),
            pl.BlockSpec((tm, n_gate), lambda i: (i, 0)),
        ],
        out_shape=[
            jax.ShapeDtypeStruct((T, n_qkv), BF16),
            jax.ShapeDtypeStruct((T, d_pool), F32),
            jax.ShapeDtypeStruct((T, n_gate), BF16),
        ],
        compiler_params=pltpu.CompilerParams(
            dimension_semantics=("arbitrary",), vmem_limit_bytes=VMEM_LIMIT),
        name="prenorm_in_proj",
    )(x2d, g.reshape(1, D), scale, shift, w_bf)


def _attn_kernel(q_ref, k_ref, v_ref, o_ref, acc_ref, carry_ref, *, blk):
    i = pl.program_id(2)
    q = q_ref[0, 0]
    row = lax.broadcasted_iota(jnp.int32, (blk, blk), 0)
    col = lax.broadcasted_iota(jnp.int32, (blk, blk), 1)
    tri = (row > col).astype(BF16)
    causal = col < row

    def step(j, mask):
        ks = pl.multiple_of(j * blk, blk)
        k = k_ref[0, 0, pl.ds(ks, blk), :]
        v = v_ref[0, 0, pl.ds(ks, blk), :]
        z = lax.dot_general(q, k, (((1,), (1,)), ((), ())), preferred_element_type=F32)
        soft = jnp.log(1.0 + jnp.exp(-jnp.abs(z)))
        log_keep = -(jnp.maximum(z, 0.0) + soft)
        log_take = jnp.minimum(z, 0.0) - soft
        if mask is not None:
            log_keep = jnp.where(mask, log_keep, 0.0)
        later = jnp.dot(log_keep.astype(BF16), tri, preferred_element_type=F32)
        a = jnp.exp(log_take + later + carry_ref[...])
        if mask is not None:
            a = jnp.where(mask, a, 0.0)
        acc_ref[...] += jnp.dot(a.astype(BF16), v, preferred_element_type=F32)
        carry_ref[...] += jnp.sum(log_keep, axis=-1, keepdims=True)

    acc_ref[...] = jnp.zeros_like(acc_ref)
    carry_ref[...] = jnp.zeros_like(carry_ref)
    step(i, causal)

    def body(jj, c):
        step(i - 1 - jj, None)
        return c

    lax.fori_loop(0, i, body, 0)
    o_ref[0, 0] = acc_ref[...].astype(o_ref.dtype)


def _attention(q, k, v, *, blk=256):
    B, H, S, hd = q.shape
    return pl.pallas_call(
        functools.partial(_attn_kernel, blk=blk),
        grid=(B, H, S // blk),
        in_specs=[
            pl.BlockSpec((1, 1, blk, hd), lambda b, h, i: (b, h, i, 0)),
            pl.BlockSpec((1, 1, S, hd), lambda b, h, i: (b, h, 0, 0)),
            pl.BlockSpec((1, 1, S, hd), lambda b, h, i: (b, h, 0, 0)),
        ],
        out_specs=pl.BlockSpec((1, 1, blk, hd), lambda b, h, i: (b, h, i, 0)),
        out_shape=jax.ShapeDtypeStruct((B, H, S, hd), BF16),
        scratch_shapes=[pltpu.VMEM((blk, hd), F32), pltpu.VMEM((blk, 1), F32)],
        compiler_params=pltpu.CompilerParams(
            dimension_semantics=("arbitrary", "arbitrary", "arbitrary"),
            vmem_limit_bytes=VMEM_LIMIT),
        name="sb_attention",
    )(q, k, v)


def _mixer_tail_kernel(attn_ref, u_ref, uh_ref, g_ref, x_ref,
                       poolw_ref, poolb_ref, pools_ref, wba_ref, wbp_ref, wout_ref,
                       post1_ref, gate1_ref, pre2_ref, sc2_ref, sh2_ref, rw_ref, rb_ref,
                       x1_ref, h2_ref, meta_ref, counts_ref, carry_ref, *, tm, tps):
    i = pl.program_id(0)
    D = x_ref.shape[1]
    d_pool = u_ref.shape[1]
    gdim = d_pool // POOL_GROUPS
    seq_tile = i % tps

    u = u_ref[...]
    halo = jnp.where(seq_tile == 0, 0.0, uh_ref[...])
    s = jnp.concatenate([halo, u], axis=0)
    sums = []
    span = 1
    for w in POOL_WINDOWS:
        while span < w:
            s = s + pltpu.roll(s, span, 0)
            span *= 2
        sums.append(s[POOL_HALO:])
    pos1 = (seq_tile * tm + 1 + lax.broadcasted_iota(jnp.int32, (tm, 1), 0)).astype(F32)
    colp = lax.broadcasted_iota(jnp.int32, (1, d_pool), 1)
    mean = sums[-1] / jnp.minimum(pos1, float(POOL_WINDOWS[-1]))
    for gi in range(POOL_GROUPS - 2, -1, -1):
        mean = jnp.where(colp < (gi + 1) * gdim,
                         sums[gi] / jnp.minimum(pos1, float(POOL_WINDOWS[gi])), mean)
    mixed = mean - u
    yp = jnp.dot(mixed.astype(BF16), poolw_ref[...], preferred_element_type=F32)
    pool = ((yp + poolb_ref[...]) * pools_ref[...]).astype(BF16)

    pa = jnp.dot(attn_ref[...], wba_ref[...], preferred_element_type=F32)
    pp = jnp.dot(pool, wbp_ref[...], preferred_element_type=F32)
    merged = g_ref[:, 0:D].astype(F32) * pa + g_ref[:, D:2 * D].astype(F32) * pp
    y = jnp.dot(merged.astype(BF16), wout_ref[...], preferred_element_type=F32)
    x1 = x_ref[...] + gate1_ref[0] * (y * _rms(y) * post1_ref[...])
    x1_ref[...] = x1
    h2 = (x1 * _rms(x1) * pre2_ref[...]) * (1.0 + sc2_ref[0]) + sh2_ref[0]
    h2_ref[...] = h2

    logits = jnp.dot(h2, rw_ref[...], precision=lax.Precision.HIGHEST,
                     preferred_element_type=F32) + rb_ref[...]
    lane = lax.broadcasted_iota(jnp.int32, (tm, LANES), 1).astype(F32)
    work = logits
    vals, idxs, hots = [], [], []
    for _ in range(TOP_K):
        m = jnp.max(work, axis=-1, keepdims=True)
        idx = jnp.min(jnp.where(work == m, lane, float(LANES)), axis=-1, keepdims=True)
        hot = lane == idx
        work = jnp.where(hot, -jnp.inf, work)
        vals.append(m)
        idxs.append(idx)
        hots.append(hot)
    exps = [jnp.exp(vk - vals[0]) for vk in vals]
    denom = exps[0] + exps[1] + exps[2] + exps[3]
    chosen = (hots[0] | hots[1] | hots[2] | hots[3]).astype(F32)

    @pl.when(i == 0)
    def _():
        carry_ref[...] = jnp.zeros_like(carry_ref)

    trow = lax.broadcasted_iota(jnp.int32, (tm, tm), 0)
    tcol = lax.broadcasted_iota(jnp.int32, (tm, tm), 1)
    before = (tcol < trow).astype(BF16)
    rank_all = jnp.dot(before, chosen.astype(BF16), preferred_element_type=F32) + carry_ref[...]
    meta = jnp.zeros((tm, LANES), F32)
    for kk in range(TOP_K):
        rk = jnp.sum(jnp.where(hots[kk], rank_all, 0.0), axis=-1, keepdims=True)
        meta = jnp.where(lane == float(META_E + kk), idxs[kk], meta)
        meta = jnp.where(lane == float(META_G + kk), exps[kk] / denom, meta)
        meta = jnp.where(lane == float(META_R + kk), rk, meta)
    meta_ref[...] = meta
    carry_ref[...] += jnp.sum(chosen, axis=0, keepdims=True)
    counts_ref[...] = carry_ref[...]


def _mixer_tail(attn, u, g, x2d, poolw_bd, poolb, pools, wba, wbp, wout,
                post1, gate1, pre2, sc2, sh2, rw_pad, rb_pad, *, seq, tm=256):
    T, D = x2d.shape
    d_pool = u.shape[1]
    tps = seq // tm
    hpt = tm // POOL_HALO

    def const(shape):
        return pl.BlockSpec(shape, lambda i: (0,) * len(shape))

    row = lambda w: pl.BlockSpec((tm, w), lambda i: (i, 0))
    vec = pl.BlockSpec((1, 1, D), lambda i: (i // tps, 0, 0))
    return pl.pallas_call(
        functools.partial(_mixer_tail_kernel, tm=tm, tps=tps),
        grid=(T // tm,),
        in_specs=[
            row(SB_WIDTH), row(d_pool),
            pl.BlockSpec((POOL_HALO, d_pool), lambda i: (jnp.maximum(i * hpt - 1, 0), 0)),
            row(2 * D), row(D),
            const((d_pool, d_pool)), const((1, d_pool)), const((1, d_pool)),
            const((SB_WIDTH, D)), const((d_pool, D)), const((D, D)),
            const((1, D)), vec, const((1, D)), vec, vec,
            const((D, LANES)), const((1, LANES)),
        ],
        out_specs=[row(D), row(D), row(LANES), const((1, LANES))],
        out_shape=[
            jax.ShapeDtypeStruct((T, D), F32),
            jax.ShapeDtypeStruct((T, D), F32),
            jax.ShapeDtypeStruct((T, LANES), F32),
            jax.ShapeDtypeStruct((1, LANES), F32),
        ],
        scratch_shapes=[pltpu.VMEM((1, LANES), F32)],
        compiler_params=pltpu.CompilerParams(
            dimension_semantics=("arbitrary",), vmem_limit_bytes=VMEM_LIMIT),
        name="mixer_tail",
    )(attn, u, u, g, x2d, poolw_bd, poolb, pools, wba, wbp, wout,
      post1, gate1, pre2, sc2, sh2, rw_pad, rb_pad)


def _row_copy_all(n_rows, make_copy, wait_like, chunk):
    def issue(j, c):
        make_copy(j).start()
        return c

    lax.fori_loop(0, n_rows, issue, 0, unroll=8)
    for _ in range(n_rows // chunk):
        wait_like(chunk).wait()
    if n_rows % chunk:
        wait_like(n_rows % chunk).wait()


def _dispatch_kernel(dest_ref, pad_ref, h_ref, xr_ref, zero_ref, sem, *, tm, ppt):
    i = pl.program_id(0)

    def make_copy(j):
        return pltpu.make_async_copy(
            h_ref.at[pl.ds(j // TOP_K, 1), :], xr_ref.at[pl.ds(dest_ref[j], 1), :], sem)

    def wait_like(n):
        return pltpu.make_async_copy(h_ref.at[pl.ds(0, n), :], xr_ref.at[pl.ds(0, n), :], sem)

    _row_copy_all(tm * TOP_K, make_copy, wait_like, tm)

    zero_ref[...] = jnp.zeros_like(zero_ref)

    def make_pad_copy(j):
        return pltpu.make_async_copy(
            zero_ref.at[pl.ds(0, 1), :], xr_ref.at[pl.ds(pad_ref[i * ppt + j], 1), :], sem)

    _row_copy_all(ppt, make_pad_copy, wait_like, tm)


def _dispatch(dest, pad_rows, h2, n_rows, *, tm=256):
    T, D = h2.shape
    steps = T // tm
    ppt = pad_rows.shape[0] // steps
    assert ppt * steps == pad_rows.shape[0]
    return pl.pallas_call(
        functools.partial(_dispatch_kernel, tm=tm, ppt=ppt),
        grid=(steps,),
        in_specs=[
            pl.BlockSpec((tm * TOP_K,), lambda i: (i,), memory_space=pltpu.SMEM),
            pl.BlockSpec(memory_space=pltpu.SMEM),
            pl.BlockSpec((tm, D), lambda i: (i, 0)),
        ],
        out_specs=pl.BlockSpec(memory_space=pl.ANY),
        out_shape=jax.ShapeDtypeStruct((n_rows, D), F32),
        scratch_shapes=[pltpu.VMEM((8, D), F32), pltpu.SemaphoreType.DMA],
        compiler_params=pltpu.CompilerParams(
            dimension_semantics=("arbitrary",), vmem_limit_bytes=VMEM_LIMIT),
        name="moe_dispatch",
    )(dest, pad_rows, h2)


def _ffn_kernel(be_ref, nused_ref, x_ref, wgu_ref, bgu_ref, wdn_ref, bdn_ref,
                o_ref, wgu_bf, wdn_bf):
    b = pl.program_id(0)
    de = wdn_ref.shape[2]

    @pl.when(b >= nused_ref[0])
    def _():
        o_ref[...] = jnp.zeros_like(o_ref)

    @pl.when(b < nused_ref[0])
    def _():
        prev = be_ref[jnp.maximum(b - 1, 0)]

        @pl.when((b == 0) | (be_ref[b] != prev))
        def _():
            wgu_bf[...] = wgu_ref[0, 0].astype(BF16)
            wdn_bf[...] = wdn_ref[0, 0].astype(BF16)

        x = x_ref[...].astype(BF16)
        gu = jnp.dot(x, wgu_bf[...], preferred_element_type=F32) + bgu_ref[0, 0]
        gate = jnp.minimum(gu[:, 0:de], SWIGLU_LIMIT)
        up = jnp.clip(gu[:, de:2 * de], -SWIGLU_LIMIT, SWIGLU_LIMIT)
        act = (up + 1.0) * gate * jax.nn.sigmoid(SWIGLU_ALPHA * gate)
        o_ref[...] = jnp.dot(act.astype(BF16), wdn_bf[...],
                             preferred_element_type=F32) + bdn_ref[0, 0]


def _ffn(block_e, n_used, xr, w_gu, b_gu, w_dn, b_dn, layer):
    n_rows, D = xr.shape
    L, E, _, n_gu = w_gu.shape
    de = w_dn.shape[2]
    n_blocks = n_rows // ROW_BLOCK

    def used(b, nu):
        return jnp.minimum(b, nu[0] - 1)

    grid_spec = pltpu.PrefetchScalarGridSpec(
        num_scalar_prefetch=2,
        grid=(n_blocks,),
        in_specs=[
            pl.BlockSpec((ROW_BLOCK, D), lambda b, be, nu: (used(b, nu), 0)),
            pl.BlockSpec((1, 1, D, n_gu), lambda b, be, nu: (layer, be[used(b, nu)], 0, 0)),
            pl.BlockSpec((1, 1, 1, n_gu), lambda b, be, nu: (layer, be[used(b, nu)], 0, 0)),
            pl.BlockSpec((1, 1, de, D), lambda b, be, nu: (layer, be[used(b, nu)], 0, 0)),
            pl.BlockSpec((1, 1, 1, D), lambda b, be, nu: (layer, be[used(b, nu)], 0, 0)),
        ],
        out_specs=pl.BlockSpec((ROW_BLOCK, D), lambda b, be, nu: (b, 0)),
        scratch_shapes=[pltpu.VMEM((D, n_gu), BF16), pltpu.VMEM((de, D), BF16)],
    )
    return pl.pallas_call(
        _ffn_kernel,
        grid_spec=grid_spec,
        out_shape=jax.ShapeDtypeStruct((n_rows, D), F32),
        compiler_params=pltpu.CompilerParams(
            dimension_semantics=("arbitrary",), vmem_limit_bytes=VMEM_LIMIT),
        name="moe_ffn",
    )(block_e, n_used, xr, w_gu, b_gu.reshape(L, E, 1, n_gu), w_dn, b_dn.reshape(L, E, 1, D))


def _combine_kernel(dest_ref, meta_ref, x1_ref, post2_ref, gate2_ref, yr_ref, o_ref, buf, sem, *, tm):
    def make_copy(j):
        return pltpu.make_async_copy(
            yr_ref.at[pl.ds(dest_ref[j], 1), :],
            buf.at[j % TOP_K, pl.ds(j // TOP_K, 1), :], sem)

    def wait_like(n):
        return pltpu.make_async_copy(yr_ref.at[pl.ds(0, n), :], buf.at[0, pl.ds(0, n), :], sem)

    _row_copy_all(tm * TOP_K, make_copy, wait_like, tm)
    y = meta_ref[:, META_G:META_G + 1] * buf[0]
    for kk in range(1, TOP_K):
        y = y + meta_ref[:, META_G + kk:META_G + kk + 1] * buf[kk]
    o_ref[...] = x1_ref[...] + gate2_ref[0] * (y * _rms(y) * post2_ref[...])


def _combine(dest, meta, x1, post2, gate2, yr, *, seq, tm=256):
    T, D = x1.shape
    tps = seq // tm
    return pl.pallas_call(
        functools.partial(_combine_kernel, tm=tm),
        grid=(T // tm,),
        in_specs=[
            pl.BlockSpec((tm * TOP_K,), lambda i: (i,), memory_space=pltpu.SMEM),
            pl.BlockSpec((tm, LANES), lambda i: (i, 0)),
            pl.BlockSpec((tm, D), lambda i: (i, 0)),
            pl.BlockSpec((1, D), lambda i: (0, 0)),
            pl.BlockSpec((1, 1, D), lambda i: (i // tps, 0, 0)),
            pl.BlockSpec(memory_space=pl.ANY),
        ],
        out_specs=pl.BlockSpec((tm, D), lambda i: (i, 0)),
        out_shape=jax.ShapeDtypeStruct((T, D), F32),
        scratch_shapes=[pltpu.VMEM((TOP_K, tm, D), F32), pltpu.SemaphoreType.DMA],
        compiler_params=pltpu.CompilerParams(
            dimension_semantics=("arbitrary",), vmem_limit_bytes=VMEM_LIMIT),
        name="moe_combine",
    )(dest, meta, x1, post2, gate2, yr)


def _block_diag(w):
    g, a, b = w.shape
    out = jnp.zeros((g * a, g * b), w.dtype)
    for i in range(g):
        out = out.at[i * a:(i + 1) * a, i * b:(i + 1) * b].set(w[i])
    return out


def kernel(x, c, ada_w, ada_b, pre1_g, post1_g, pre2_g, post2_g, w_in, pool_w, pool_b, pool_scale,
           w_br_attn, w_br_pool, w_out, router_w, router_b, w_gu, b_gu, w_dn, b_dn):
    B, S, D = x.shape
    L = ada_w.shape[0]
    T = B * S
    d_pool = D - SB_WIDTH
    n_rows = T * TOP_K + N_EXPERTS * ROW_BLOCK
    n_blocks = n_rows // ROW_BLOCK

    c_pad = jnp.pad(c, ((0, 8 - B), (0, 0)))
    mod = _ada(c_pad, ada_w, ada_b)[:, :B]

    x2d = x.reshape(T, D)
    for l in range(L):
        shift1, scale1, gate1, shift2, scale2, gate2 = [
            mod[l, :, i * D:(i + 1) * D].reshape(B, 1, D) for i in range(6)]

        qkv, u, g = _in_proj(x2d, pre1_g[l], scale1, shift1, w_in[l].astype(BF16), seq=S)
        qkv = qkv.reshape(B, S, 3, SB_HEADS, SB_HEAD_DIM).transpose(2, 0, 3, 1, 4)
        attn = _attention(qkv[0], qkv[1], qkv[2])
        attn = attn.transpose(0, 2, 1, 3).reshape(T, SB_WIDTH)

        rw_pad = jnp.pad(router_w[l], ((0, 0), (0, LANES - N_EXPERTS)))
        rb_pad = jnp.pad(router_b[l], (0, LANES - N_EXPERTS), constant_values=-jnp.inf)
        x1, h2, meta, counts = _mixer_tail(
            attn, u, g, x2d,
            _block_diag(pool_w[l]).astype(BF16), pool_b[l].reshape(1, d_pool),
            pool_scale[l].reshape(1, d_pool),
            w_br_attn[l].astype(BF16), w_br_pool[l].astype(BF16), w_out[l].astype(BF16),
            post1_g[l].reshape(1, D), gate1, pre2_g[l].reshape(1, D), scale2, shift2,
            rw_pad, rb_pad.reshape(1, LANES), seq=S)

        e_idx = meta[:, META_E:META_E + TOP_K].astype(jnp.int32)
        rank = meta[:, META_R:META_R + TOP_K].astype(jnp.int32)
        cnt = counts[0, :N_EXPERTS].astype(jnp.int32)
        padded = ((cnt + ROW_BLOCK - 1) // ROW_BLOCK) * ROW_BLOCK
        pend = jnp.cumsum(padded)
        pstart = pend - padded
        dest = (pstart[e_idx] + rank).reshape(-1)
        blk0 = jnp.arange(n_blocks, dtype=jnp.int32) * ROW_BLOCK
        block_e = jnp.minimum(jnp.searchsorted(pend, blk0, side="right"), N_EXPERTS - 1).astype(jnp.int32)
        n_used = (pend[-1:] // ROW_BLOCK).astype(jnp.int32)
        gap = padded - cnt
        gap_end = jnp.cumsum(gap)
        p = jnp.arange(n_rows - T * TOP_K, dtype=jnp.int32)
        pe = jnp.searchsorted(gap_end, p, side="right")
        pe_c = jnp.minimum(pe, N_EXPERTS - 1)
        in_expert = pstart[pe_c] + cnt[pe_c] + (p - (gap_end - gap)[pe_c])
        pad_rows = jnp.where(pe < N_EXPERTS, in_expert, pend[-1] + (p - gap_end[-1])).astype(jnp.int32)

        xr = _dispatch(dest, pad_rows, h2, n_rows)
        yr = _ffn(block_e, n_used, xr, w_gu, b_gu, w_dn, b_dn, l)
        x2d = _combine(dest, meta, x1, post2_g[l].reshape(1, D), gate2, yr, seq=S)
    return x2d.reshape(B, S, D)
```

```python
import functools

import jax
import jax.numpy as jnp
from jax import lax
from jax.experimental import pallas as pl
from jax.experimental.pallas import tpu as pltpu

F32 = jnp.float32
BF16 = jnp.bfloat16

SB_HEADS = 4
SB_HEAD_DIM = 64
SB_WIDTH = SB_HEADS * SB_HEAD_DIM
POOL_WINDOWS = (2, 4, 8, 16)
POOL_GROUPS = len(POOL_WINDOWS)
POOL_HALO = 16
N_EXPERTS = 32
TOP_K = 4
ROW_BLOCK = 256
SWIGLU_LIMIT = 7.0
SWIGLU_ALPHA = 1.702
NORM_EPS = 1e-6
LOG2_E = 1.4426950408889634

LANES = 128
VMEM_LIMIT = 52 * 1024 * 1024

META_E, META_G, META_R = 0, 4, 8


def _rms(x):
    return lax.rsqrt(jnp.mean(x * x, axis=-1, keepdims=True) + NORM_EPS)


def _ada_kernel(c_ref, w_ref, b_ref, o_ref):
    c = c_ref[...]
    ca = c * jax.nn.sigmoid(c)
    o_ref[0] = jnp.dot(ca, w_ref[0], precision=lax.Precision.HIGHEST,
                       preferred_element_type=F32) + b_ref[0]


def _ada(c_pad, ada_w, ada_b):
    L, D, N = ada_w.shape
    bp = c_pad.shape[0]
    tn = 1536
    return pl.pallas_call(
        _ada_kernel,
        grid=(L, N // tn),
        in_specs=[
            pl.BlockSpec((bp, D), lambda l, j: (0, 0)),
            pl.BlockSpec((1, D, tn), lambda l, j: (l, 0, j)),
            pl.BlockSpec((1, 1, tn), lambda l, j: (l, 0, j)),
        ],
        out_specs=pl.BlockSpec((1, bp, tn), lambda l, j: (l, 0, j)),
        out_shape=jax.ShapeDtypeStruct((L, bp, N), F32),
        compiler_params=pltpu.CompilerParams(
            dimension_semantics=("arbitrary", "arbitrary"), vmem_limit_bytes=VMEM_LIMIT),
        name="ada_mod",
    )(c_pad, ada_w, ada_b.reshape(L, 1, N))


def _in_proj_kernel(x_ref, g_ref, sc_ref, sh_ref, w_ref, qkv_ref, u_ref, gate_ref, *, d_pool):
    x = x_ref[...]
    h = (x * _rms(x) * g_ref[...]) * (1.0 + sc_ref[0]) + sh_ref[0]
    hb = h.astype(BF16)
    n_qkv = 3 * SB_WIDTH
    qkv = jnp.dot(hb, w_ref[:, 0:n_qkv], preferred_element_type=F32)
    col = lax.broadcasted_iota(jnp.int32, (1, n_qkv), 1)
    qkv = jnp.where(col < SB_WIDTH, qkv * (LOG2_E * SB_HEAD_DIM ** -0.5), qkv)
    qkv_ref[...] = qkv.astype(BF16)
    u_ref[...] = jnp.dot(hb, w_ref[:, n_qkv:n_qkv + d_pool], preferred_element_type=F32)
    gl = jnp.dot(hb, w_ref[:, n_qkv + d_pool:], preferred_element_type=F32)
    gate_ref[...] = jax.nn.sigmoid(gl).astype(BF16)


def _in_proj(x2d, g, scale, shift, w_bf, *, seq, tm=512):
    T, D = x2d.shape
    n_in = w_bf.shape[1]
    n_qkv = 3 * SB_WIDTH
    d_pool = D - SB_WIDTH
    n_gate = n_in - n_qkv - d_pool
    tps = seq // tm
    vec = pl.BlockSpec((1, 1, D), lambda i: (i // tps, 0, 0))
    return pl.pallas_call(
        functools.partial(_in_proj_kernel, d_pool=d_pool),
        grid=(T // tm,),
        in_specs=[
            pl.BlockSpec((tm, D), lambda i: (i, 0)),
            pl.BlockSpec((1, D), lambda i: (0, 0)),
            vec, vec,
            pl.BlockSpec((D, n_in), lambda i: (0, 0)),
        ],
        out_specs=[
            pl.BlockSpec((tm, n_qkv), lambda i: (i, 0)),
            pl.BlockSpec((tm, d_pool), lambda i: (i, 0)),
            pl.BlockSpec((tm, n_gate), lambda i: (i, 0)),
        ],
        out_shape=[
            jax.ShapeDtypeStruct((T, n_qkv), BF16),
            jax.ShapeDtypeStruct((T, d_pool), F32),
            jax.ShapeDtypeStruct((T, n_gate), BF16),
        ],
        compiler_params=pltpu.CompilerParams(
            dimension_semantics=("arbitrary",), vmem_limit_bytes=VMEM_LIMIT),
        name="prenorm_in_proj",
    )(x2d, g.reshape(1, D), scale, shift, w_bf)


def _attn_kernel(q_ref, k_ref, v_ref, o_ref, acc_ref, carry_ref, *, blk):
    i = pl.program_id(1)
    width = q_ref.shape[2]
    heads = width // SB_HEAD_DIM
    head_of_lane = lax.broadcasted_iota(jnp.int32, (1, width), 1) // SB_HEAD_DIM
    q = q_ref[0]
    zero = jnp.zeros((), BF16)
    qs = jnp.concatenate([jnp.where(head_of_lane == h, q, zero) for h in range(heads)], axis=0)
    row = lax.broadcasted_iota(jnp.int32, (blk, blk), 0)
    col = lax.broadcasted_iota(jnp.int32, (blk, blk), 1)
    tri = (row > col).astype(BF16)
    causal = jnp.concatenate([col < row] * heads, axis=0)

    def step(j, mask):
        ks = pl.multiple_of(j * blk, blk)
        k = k_ref[0, pl.ds(ks, blk), :]
        v = v_ref[0, pl.ds(ks, blk), :]
        t = lax.dot_general(qs, k, (((1,), (1,)), ((), ())), preferred_element_type=F32)
        soft = jnp.log2(1.0 + jnp.exp2(-jnp.abs(t)))
        log_take = jnp.minimum(t, 0.0) - soft
        log_keep = log_take - t
        if mask is not None:
            log_keep = jnp.where(mask, log_keep, 0.0)
        later = jnp.dot(log_keep.astype(BF16), tri, preferred_element_type=F32)
        a = jnp.exp2(log_take + later + carry_ref[...])
        if mask is not None:
            a = jnp.where(mask, a, 0.0)
        a = a.astype(BF16)
        a_wide = jnp.concatenate([a[h * blk:(h + 1) * blk] for h in range(heads)], axis=1)
        v_heads = jnp.concatenate([jnp.where(head_of_lane == h, v, zero) for h in range(heads)], axis=0)
        acc_ref[...] += jnp.dot(a_wide, v_heads, preferred_element_type=F32)
        carry_ref[...] += jnp.sum(log_keep, axis=-1, keepdims=True)

    acc_ref[...] = jnp.zeros_like(acc_ref)
    carry_ref[...] = jnp.zeros_like(carry_ref)
    step(i, causal)

    def body(jj, c):
        step(i - 1 - jj, None)
        return c

    lax.fori_loop(0, i, body, 0)
    o_ref[0] = acc_ref[...].astype(o_ref.dtype)


def _attention(qkv, *, blk=256):
    B, S, _ = qkv.shape
    whole = lambda part: pl.BlockSpec((1, S, SB_WIDTH), lambda b, i: (b, 0, part),
                                      pipeline_mode=pl.Buffered(1))
    return pl.pallas_call(
        functools.partial(_attn_kernel, blk=blk),
        grid=(B, S // blk),
        in_specs=[
            pl.BlockSpec((1, blk, SB_WIDTH), lambda b, i: (b, i, 0)),
            whole(1), whole(2),
        ],
        out_specs=pl.BlockSpec((1, blk, SB_WIDTH), lambda b, i: (b, i, 0)),
        out_shape=jax.ShapeDtypeStruct((B, S, SB_WIDTH), BF16),
        scratch_shapes=[pltpu.VMEM((blk, SB_WIDTH), F32),
                        pltpu.VMEM((SB_HEADS * blk, 1), F32)],
        compiler_params=pltpu.CompilerParams(
            dimension_semantics=("arbitrary", "arbitrary"), vmem_limit_bytes=VMEM_LIMIT),
        name="sb_attention",
    )(qkv, qkv, qkv)


def _mixer_tail_kernel(attn_ref, u_ref, uh_ref, g_ref, x_ref,
                       poolw_ref, poolb_ref, pools_ref, wba_ref, wbp_ref, wout_ref,
                       post1_ref, gate1_ref, pre2_ref, sc2_ref, sh2_ref, rw_ref, rb_ref,
                       x1_ref, h2_ref, meta_ref, counts_ref, carry_ref, *, tm, tps):
    i = pl.program_id(0)
    D = x_ref.shape[1]
    d_pool = u_ref.shape[1]
    gdim = d_pool // POOL_GROUPS
    seq_tile = i % tps

    u = u_ref[...]
    halo = jnp.where(seq_tile == 0, 0.0, uh_ref[...])
    s = jnp.concatenate([halo, u], axis=0)
    sums = []
    span = 1
    for w in POOL_WINDOWS:
        while span < w:
            s = s + pltpu.roll(s, span, 0)
            span *= 2
        sums.append(s[POOL_HALO:])
    pos1 = (seq_tile * tm + 1 + lax.broadcasted_iota(jnp.int32, (tm, 1), 0)).astype(F32)
    colp = lax.broadcasted_iota(jnp.int32, (1, d_pool), 1)
    mean = sums[-1] / jnp.minimum(pos1, float(POOL_WINDOWS[-1]))
    for gi in range(POOL_GROUPS - 2, -1, -1):
        mean = jnp.where(colp < (gi + 1) * gdim,
                         sums[gi] / jnp.minimum(pos1, float(POOL_WINDOWS[gi])), mean)
    mixed = mean - u
    yp = jnp.dot(mixed.astype(BF16), poolw_ref[...], preferred_element_type=F32)
    pool = ((yp + poolb_ref[...]) * pools_ref[...]).astype(BF16)

    pa = jnp.dot(attn_ref[...], wba_ref[...], preferred_element_type=F32)
    pp = jnp.dot(pool, wbp_ref[...], preferred_element_type=F32)
    merged = g_ref[:, 0:D].astype(F32) * pa + g_ref[:, D:2 * D].astype(F32) * pp
    y = jnp.dot(merged.astype(BF16), wout_ref[...], preferred_element_type=F32)
    x1 = x_ref[...] + gate1_ref[0] * (y * _rms(y) * post1_ref[...])
    x1_ref[...] = x1
    h2 = (x1 * _rms(x1) * pre2_ref[...]) * (1.0 + sc2_ref[0]) + sh2_ref[0]
    h2_ref[...] = h2

    logits = jnp.dot(h2, rw_ref[...], precision=lax.Precision.HIGHEST,
                     preferred_element_type=F32) + rb_ref[...]
    lane = lax.broadcasted_iota(jnp.int32, (tm, LANES), 1).astype(F32)
    work = logits
    vals, idxs, hots = [], [], []
    for _ in range(TOP_K):
        m = jnp.max(work, axis=-1, keepdims=True)
        idx = jnp.min(jnp.where(work == m, lane, float(LANES)), axis=-1, keepdims=True)
        hot = lane == idx
        work = jnp.where(hot, -jnp.inf, work)
        vals.append(m)
        idxs.append(idx)
        hots.append(hot)
    exps = [jnp.exp(vk - vals[0]) for vk in vals]
    denom = exps[0] + exps[1] + exps[2] + exps[3]
    chosen = (hots[0] | hots[1] | hots[2] | hots[3]).astype(F32)

    @pl.when(i == 0)
    def _():
        carry_ref[...] = jnp.zeros_like(carry_ref)

    trow = lax.broadcasted_iota(jnp.int32, (tm, tm), 0)
    tcol = lax.broadcasted_iota(jnp.int32, (tm, tm), 1)
    before = (tcol < trow).astype(BF16)
    rank_all = jnp.dot(before, chosen.astype(BF16), preferred_element_type=F32) + carry_ref[...]
    meta = jnp.zeros((tm, LANES), F32)
    for kk in range(TOP_K):
        rk = jnp.sum(jnp.where(hots[kk], rank_all, 0.0), axis=-1, keepdims=True)
        meta = jnp.where(lane == float(META_E + kk), idxs[kk], meta)
        meta = jnp.where(lane == float(META_G + kk), exps[kk] / denom, meta)
        meta = jnp.where(lane == float(META_R + kk), rk, meta)
    meta_ref[...] = meta
    carry_ref[...] += jnp.sum(chosen, axis=0, keepdims=True)
    counts_ref[...] = carry_ref[...]


def _mixer_tail(attn, u, g, x2d, poolw_bd, poolb, pools, wba, wbp, wout,
                post1, gate1, pre2, sc2, sh2, rw_pad, rb_pad, *, seq, tm=256):
    T, D = x2d.shape
    d_pool = u.shape[1]
    tps = seq // tm
    hpt = tm // POOL_HALO

    def const(shape):
        return pl.BlockSpec(shape, lambda i: (0,) * len(shape))

    row = lambda w: pl.BlockSpec((tm, w), lambda i: (i, 0))
    vec = pl.BlockSpec((1, 1, D), lambda i: (i // tps, 0, 0))
    return pl.pallas_call(
        functools.partial(_mixer_tail_kernel, tm=tm, tps=tps),
        grid=(T // tm,),
        in_specs=[
            row(SB_WIDTH), row(d_pool),
            pl.BlockSpec((POOL_HALO, d_pool), lambda i: (jnp.maximum(i * hpt - 1, 0), 0)),
            row(2 * D), row(D),
            const((d_pool, d_pool)), const((1, d_pool)), const((1, d_pool)),
            const((SB_WIDTH, D)), const((d_pool, D)), const((D, D)),
            const((1, D)), vec, const((1, D)), vec, vec,
            const((D, LANES)), const((1, LANES)),
        ],
        out_specs=[row(D), row(D), row(LANES), const((1, LANES))],
        out_shape=[
            jax.ShapeDtypeStruct((T, D), F32),
            jax.ShapeDtypeStruct((T, D), F32),
            jax.ShapeDtypeStruct((T, LANES), F32),
            jax.ShapeDtypeStruct((1, LANES), F32),
        ],
        scratch_shapes=[pltpu.VMEM((1, LANES), F32)],
        compiler_params=pltpu.CompilerParams(
            dimension_semantics=("arbitrary",), vmem_limit_bytes=VMEM_LIMIT),
        name="mixer_tail",
    )(attn, u, u, g, x2d, poolw_bd, poolb, pools, wba, wbp, wout,
      post1, gate1, pre2, sc2, sh2, rw_pad, rb_pad)


def _row_copy_all(n_rows, make_copy, wait_like, chunk):
    def issue(j, c):
        make_copy(j).start()
        return c

    lax.fori_loop(0, n_rows, issue, 0, unroll=8)
    for _ in range(n_rows // chunk):
        wait_like(chunk).wait()
    if n_rows % chunk:
        wait_like(n_rows % chunk).wait()


def _dispatch_kernel(dest_ref, pad_ref, h_ref, xr_ref, zero_ref, sem, *, tm, ppt):
    i = pl.program_id(0)

    def make_copy(j):
        return pltpu.make_async_copy(
            h_ref.at[pl.ds(j // TOP_K, 1), :], xr_ref.at[pl.ds(dest_ref[j], 1), :], sem)

    def wait_like(n):
        return pltpu.make_async_copy(h_ref.at[pl.ds(0, n), :], xr_ref.at[pl.ds(0, n), :], sem)

    _row_copy_all(tm * TOP_K, make_copy, wait_like, tm)

    zero_ref[...] = jnp.zeros_like(zero_ref)

    def make_pad_copy(j):
        return pltpu.make_async_copy(
            zero_ref.at[pl.ds(0, 1), :], xr_ref.at[pl.ds(pad_ref[i * ppt + j], 1), :], sem)

    _row_copy_all(ppt, make_pad_copy, wait_like, tm)


def _dispatch(dest, pad_rows, h2, n_rows, *, tm=256):
    T, D = h2.shape
    steps = T // tm
    ppt = pad_rows.shape[0] // steps
    assert ppt * steps == pad_rows.shape[0]
    return pl.pallas_call(
        functools.partial(_dispatch_kernel, tm=tm, ppt=ppt),
        grid=(steps,),
        in_specs=[
            pl.BlockSpec((tm * TOP_K,), lambda i: (i,), memory_space=pltpu.SMEM),
            pl.BlockSpec(memory_space=pltpu.SMEM),
            pl.BlockSpec((tm, D), lambda i: (i, 0)),
        ],
        out_specs=pl.BlockSpec(memory_space=pl.ANY),
        out_shape=jax.ShapeDtypeStruct((n_rows, D), F32),
        scratch_shapes=[pltpu.VMEM((8, D), F32), pltpu.SemaphoreType.DMA],
        compiler_params=pltpu.CompilerParams(
            dimension_semantics=("arbitrary",), vmem_limit_bytes=VMEM_LIMIT),
        name="moe_dispatch",
    )(dest, pad_rows, h2)


def _ffn_kernel(be_ref, nused_ref, x_ref, wgu_ref, bgu_ref, wdn_ref, bdn_ref,
                o_ref, wgu_bf, wdn_bf):
    b = pl.program_id(0)
    de = wdn_ref.shape[2]

    @pl.when(b >= nused_ref[0])
    def _():
        o_ref[...] = jnp.zeros_like(o_ref)

    @pl.when(b < nused_ref[0])
    def _():
        prev = be_ref[jnp.maximum(b - 1, 0)]

        @pl.when((b == 0) | (be_ref[b] != prev))
        def _():
            wgu_bf[...] = wgu_ref[0, 0].astype(BF16)
            wdn_bf[...] = wdn_ref[0, 0].astype(BF16)

        x = x_ref[...].astype(BF16)
        gu = jnp.dot(x, wgu_bf[...], preferred_element_type=F32) + bgu_ref[0, 0]
        gate = jnp.minimum(gu[:, 0:de], SWIGLU_LIMIT)
        up = jnp.clip(gu[:, de:2 * de], -SWIGLU_LIMIT, SWIGLU_LIMIT)
        act = (up + 1.0) * gate * jax.nn.sigmoid(SWIGLU_ALPHA * gate)
        o_ref[...] = jnp.dot(act.astype(BF16), wdn_bf[...],
                             preferred_element_type=F32) + bdn_ref[0, 0]


def _ffn(block_e, n_used, xr, w_gu, b_gu, w_dn, b_dn, layer):
    n_rows, D = xr.shape
    L, E, _, n_gu = w_gu.shape
    de = w_dn.shape[2]
    n_blocks = n_rows // ROW_BLOCK

    def used(b, nu):
        return jnp.minimum(b, nu[0] - 1)

    grid_spec = pltpu.PrefetchScalarGridSpec(
        num_scalar_prefetch=2,
        grid=(n_blocks,),
        in_specs=[
            pl.BlockSpec((ROW_BLOCK, D), lambda b, be, nu: (used(b, nu), 0)),
            pl.BlockSpec((1, 1, D, n_gu), lambda b, be, nu: (layer, be[used(b, nu)], 0, 0)),
            pl.BlockSpec((1, 1, 1, n_gu), lambda b, be, nu: (layer, be[used(b, nu)], 0, 0)),
            pl.BlockSpec((1, 1, de, D), lambda b, be, nu: (layer, be[used(b, nu)], 0, 0)),
            pl.BlockSpec((1, 1, 1, D), lambda b, be, nu: (layer, be[used(b, nu)], 0, 0)),
        ],
        out_specs=pl.BlockSpec((ROW_BLOCK, D), lambda b, be, nu: (b, 0)),
        scratch_shapes=[pltpu.VMEM((D, n_gu), BF16), pltpu.VMEM((de, D), BF16)],
    )
    return pl.pallas_call(
        _ffn_kernel,
        grid_spec=grid_spec,
        out_shape=jax.ShapeDtypeStruct((n_rows, D), F32),
        compiler_params=pltpu.CompilerParams(
            dimension_semantics=("arbitrary",), vmem_limit_bytes=VMEM_LIMIT),
        name="moe_ffn",
    )(block_e, n_used, xr, w_gu, b_gu.reshape(L, E, 1, n_gu), w_dn, b_dn.reshape(L, E, 1, D))


def _combine_kernel(dest_ref, meta_ref, x1_ref, post2_ref, gate2_ref, yr_ref, o_ref, buf, sem, *, tm):
    def make_copy(j):
        return pltpu.make_async_copy(
            yr_ref.at[pl.ds(dest_ref[j], 1), :],
            buf.at[j % TOP_K, pl.ds(j // TOP_K, 1), :], sem)

    def wait_like(n):
        return pltpu.make_async_copy(yr_ref.at[pl.ds(0, n), :], buf.at[0, pl.ds(0, n), :], sem)

    _row_copy_all(tm * TOP_K, make_copy, wait_like, tm)
    y = meta_ref[:, META_G:META_G + 1] * buf[0]
    for kk in range(1, TOP_K):
        y = y + meta_ref[:, META_G + kk:META_G + kk + 1] * buf[kk]
    o_ref[...] = x1_ref[...] + gate2_ref[0] * (y * _rms(y) * post2_ref[...])


def _combine(dest, meta, x1, post2, gate2, yr, *, seq, tm=256):
    T, D = x1.shape
    tps = seq // tm
    return pl.pallas_call(
        functools.partial(_combine_kernel, tm=tm),
        grid=(T // tm,),
        in_specs=[
            pl.BlockSpec((tm * TOP_K,), lambda i: (i,), memory_space=pltpu.SMEM),
            pl.BlockSpec((tm, LANES), lambda i: (i, 0)),
            pl.BlockSpec((tm, D), lambda i: (i, 0)),
            pl.BlockSpec((1, D), lambda i: (0, 0)),
            pl.BlockSpec((1, 1, D), lambda i: (i // tps, 0, 0)),
            pl.BlockSpec(memory_space=pl.ANY),
        ],
        out_specs=pl.BlockSpec((tm, D), lambda i: (i, 0)),
        out_shape=jax.ShapeDtypeStruct((T, D), F32),
        scratch_shapes=[pltpu.VMEM((TOP_K, tm, D), F32), pltpu.SemaphoreType.DMA],
        compiler_params=pltpu.CompilerParams(
            dimension_semantics=("arbitrary",), vmem_limit_bytes=VMEM_LIMIT),
        name="moe_combine",
    )(dest, meta, x1, post2, gate2, yr)


def _block_diag(w):
    g, a, b = w.shape
    out = jnp.zeros((g * a, g * b), w.dtype)
    for i in range(g):
        out = out.at[i * a:(i + 1) * a, i * b:(i + 1) * b].set(w[i])
    return out


def kernel(x, c, ada_w, ada_b, pre1_g, post1_g, pre2_g, post2_g, w_in, pool_w, pool_b, pool_scale,
           w_br_attn, w_br_pool, w_out, router_w, router_b, w_gu, b_gu, w_dn, b_dn):
    B, S, D = x.shape
    L = ada_w.shape[0]
    T = B * S
    d_pool = D - SB_WIDTH
    n_rows = T * TOP_K + N_EXPERTS * ROW_BLOCK
    n_blocks = n_rows // ROW_BLOCK

    c_pad = jnp.pad(c, ((0, 8 - B), (0, 0)))
    mod = _ada(c_pad, ada_w, ada_b)[:, :B]

    x2d = x.reshape(T, D)
    for l in range(L):
        shift1, scale1, gate1, shift2, scale2, gate2 = [
            mod[l, :, i * D:(i + 1) * D].reshape(B, 1, D) for i in range(6)]

        qkv, u, g = _in_proj(x2d, pre1_g[l], scale1, shift1, w_in[l].astype(BF16), seq=S)
        attn = _attention(qkv.reshape(B, S, 3 * SB_WIDTH)).reshape(T, SB_WIDTH)

        rw_pad = jnp.pad(router_w[l], ((0, 0), (0, LANES - N_EXPERTS)))
        rb_pad = jnp.pad(router_b[l], (0, LANES - N_EXPERTS), constant_values=-jnp.inf)
        x1, h2, meta, counts = _mixer_tail(
            attn, u, g, x2d,
            _block_diag(pool_w[l]).astype(BF16), pool_b[l].reshape(1, d_pool),
            pool_scale[l].reshape(1, d_pool),
            w_br_attn[l].astype(BF16), w_br_pool[l].astype(BF16), w_out[l].astype(BF16),
            post1_g[l].reshape(1, D), gate1, pre2_g[l].reshape(1, D), scale2, shift2,
            rw_pad, rb_pad.reshape(1, LANES), seq=S)

        e_idx = meta[:, META_E:META_E + TOP_K].astype(jnp.int32)
        rank = meta[:, META_R:META_R + TOP_K].astype(jnp.int32)
        cnt = counts[0, :N_EXPERTS].astype(jnp.int32)
        padded = ((cnt + ROW_BLOCK - 1) // ROW_BLOCK) * ROW_BLOCK
        pend = jnp.cumsum(padded)
        pstart = pend - padded
        dest = (pstart[e_idx] + rank).reshape(-1)
        blk0 = jnp.arange(n_blocks, dtype=jnp.int32) * ROW_BLOCK
        block_e = jnp.minimum(jnp.sum(pend[None, :] <= blk0[:, None], axis=1), N_EXPERTS - 1).astype(jnp.int32)
        n_used = (pend[-1:] // ROW_BLOCK).astype(jnp.int32)
        gap = padded - cnt
        gap_end = jnp.cumsum(gap)
        p = jnp.arange(n_rows - T * TOP_K, dtype=jnp.int32)
        pe = jnp.sum(gap_end[None, :] <= p[:, None], axis=1)
        pe_c = jnp.minimum(pe, N_EXPERTS - 1)
        in_expert = pstart[pe_c] + cnt[pe_c] + (p - (gap_end - gap)[pe_c])
        pad_rows = jnp.where(pe < N_EXPERTS, in_expert, pend[-1] + (p - gap_end[-1])).astype(jnp.int32)

        xr = _dispatch(dest, pad_rows, h2, n_rows)
        yr = _ffn(block_e, n_used, xr, w_gu, b_gu, w_dn, b_dn, l)
        x2d = _combine(dest, meta, x1, post2_g[l].reshape(1, D), gate2, yr, seq=S)
    return x2d.reshape(B, S, D)
```

```python
import functools

import jax
import jax.numpy as jnp
from jax import lax
from jax.experimental import pallas as pl
from jax.experimental.pallas import tpu as pltpu

F32 = jnp.float32
BF16 = jnp.bfloat16

SB_HEADS = 4
SB_HEAD_DIM = 64
SB_WIDTH = SB_HEADS * SB_HEAD_DIM
POOL_WINDOWS = (2, 4, 8, 16)
POOL_GROUPS = len(POOL_WINDOWS)
POOL_HALO = 16
N_EXPERTS = 32
TOP_K = 4
ROW_BLOCK = 256
SWIGLU_LIMIT = 7.0
SWIGLU_ALPHA = 1.702
NORM_EPS = 1e-6
LOG2_E = 1.4426950408889634
F32_UNDERFLOW_LOG2 = -150.0

LANES = 128
VMEM_LIMIT = 52 * 1024 * 1024

META_E, META_G, META_R = 0, 4, 8


def _rms(x):
    return lax.rsqrt(jnp.mean(x * x, axis=-1, keepdims=True) + NORM_EPS)


def _ada_kernel(c_ref, w_ref, b_ref, o_ref):
    c = c_ref[...]
    ca = c * jax.nn.sigmoid(c)
    o_ref[0] = jnp.dot(ca, w_ref[0], precision=lax.Precision.HIGHEST,
                       preferred_element_type=F32) + b_ref[0]


def _ada(c_pad, ada_w, ada_b):
    L, D, N = ada_w.shape
    bp = c_pad.shape[0]
    tn = 1536
    return pl.pallas_call(
        _ada_kernel,
        grid=(L, N // tn),
        in_specs=[
            pl.BlockSpec((bp, D), lambda l, j: (0, 0)),
            pl.BlockSpec((1, D, tn), lambda l, j: (l, 0, j)),
            pl.BlockSpec((1, 1, tn), lambda l, j: (l, 0, j)),
        ],
        out_specs=pl.BlockSpec((1, bp, tn), lambda l, j: (l, 0, j)),
        out_shape=jax.ShapeDtypeStruct((L, bp, N), F32),
        compiler_params=pltpu.CompilerParams(
            dimension_semantics=("arbitrary", "arbitrary"), vmem_limit_bytes=VMEM_LIMIT),
        name="ada_mod",
    )(c_pad, ada_w, ada_b.reshape(L, 1, N))


def _in_proj_kernel(x_ref, g_ref, sc_ref, sh_ref, w_ref, qkv_ref, u_ref, gate_ref, *, d_pool):
    x = x_ref[...]
    h = (x * _rms(x) * g_ref[...]) * (1.0 + sc_ref[0]) + sh_ref[0]
    hb = h.astype(BF16)
    n_qkv = 3 * SB_WIDTH
    qkv = jnp.dot(hb, w_ref[:, 0:n_qkv], preferred_element_type=F32)
    col = lax.broadcasted_iota(jnp.int32, (1, n_qkv), 1)
    qkv = jnp.where(col < SB_WIDTH, qkv * (LOG2_E * SB_HEAD_DIM ** -0.5), qkv)
    qkv_ref[...] = qkv.astype(BF16)
    u_ref[...] = jnp.dot(hb, w_ref[:, n_qkv:n_qkv + d_pool], preferred_element_type=F32)
    gl = jnp.dot(hb, w_ref[:, n_qkv + d_pool:], preferred_element_type=F32)
    gate_ref[...] = jax.nn.sigmoid(gl).astype(BF16)


def _in_proj(x2d, g, scale, shift, w_bf, *, seq, tm=512):
    T, D = x2d.shape
    n_in = w_bf.shape[1]
    n_qkv = 3 * SB_WIDTH
    d_pool = D - SB_WIDTH
    n_gate = n_in - n_qkv - d_pool
    tps = seq // tm
    vec = pl.BlockSpec((1, 1, D), lambda i: (i // tps, 0, 0))
    return pl.pallas_call(
        functools.partial(_in_proj_kernel, d_pool=d_pool),
        grid=(T // tm,),
        in_specs=[
            pl.BlockSpec((tm, D), lambda i: (i, 0)),
            pl.BlockSpec((1, D), lambda i: (0, 0)),
            vec, vec,
            pl.BlockSpec((D, n_in), lambda i: (0, 0)),
        ],
        out_specs=[
            pl.BlockSpec((tm, n_qkv), lambda i: (i, 0)),
            pl.BlockSpec((tm, d_pool), lambda i: (i, 0)),
            pl.BlockSpec((tm, n_gate), lambda i: (i, 0)),
        ],
        out_shape=[
            jax.ShapeDtypeStruct((T, n_qkv), BF16),
            jax.ShapeDtypeStruct((T, d_pool), F32),
            jax.ShapeDtypeStruct((T, n_gate), BF16),
        ],
        compiler_params=pltpu.CompilerParams(
            dimension_semantics=("arbitrary",), vmem_limit_bytes=VMEM_LIMIT),
        name="prenorm_in_proj",
    )(x2d, g.reshape(1, D), scale, shift, w_bf)


def _attn_kernel(q_ref, k_ref, v_ref, o_ref, acc_ref, carry_ref, *, blk):
    i = pl.program_id(1)
    width = q_ref.shape[2]
    heads = width // SB_HEAD_DIM
    head_of_lane = lax.broadcasted_iota(jnp.int32, (1, width), 1) // SB_HEAD_DIM
    q = q_ref[0]
    zero = jnp.zeros((), BF16)
    qs = jnp.concatenate([jnp.where(head_of_lane == h, q, zero) for h in range(heads)], axis=0)
    row = lax.broadcasted_iota(jnp.int32, (blk, blk), 0)
    col = lax.broadcasted_iota(jnp.int32, (blk, blk), 1)
    tri = (row > col).astype(BF16)
    causal = jnp.concatenate([col < row] * heads, axis=0)

    def step(j, mask):
        ks = pl.multiple_of(j * blk, blk)
        k = k_ref[0, pl.ds(ks, blk), :]
        v = v_ref[0, pl.ds(ks, blk), :]
        t = lax.dot_general(qs, k, (((1,), (1,)), ((), ())), preferred_element_type=F32)
        soft = jnp.log2(1.0 + jnp.exp2(-jnp.abs(t)))
        log_take = jnp.minimum(t, 0.0) - soft
        log_keep = log_take - t
        if mask is not None:
            log_keep = jnp.where(mask, log_keep, 0.0)
        later = jnp.dot(log_keep.astype(BF16), tri, preferred_element_type=F32)
        a = jnp.exp2(log_take + later + carry_ref[...])
        if mask is not None:
            a = jnp.where(mask, a, 0.0)
        a = a.astype(BF16)
        a_wide = jnp.concatenate([a[h * blk:(h + 1) * blk] for h in range(heads)], axis=1)
        v_heads = jnp.concatenate([jnp.where(head_of_lane == h, v, zero) for h in range(heads)], axis=0)
        acc_ref[...] += jnp.dot(a_wide, v_heads, preferred_element_type=F32)
        carry_ref[...] += jnp.sum(log_keep, axis=-1, keepdims=True)

    acc_ref[...] = jnp.zeros_like(acc_ref)
    carry_ref[...] = jnp.zeros_like(carry_ref)
    step(i, causal)

    def more(state):
        jj, carry_max = state
        return jnp.logical_and(jj < i, carry_max > F32_UNDERFLOW_LOG2)

    def body(state):
        jj, _ = state
        step(i - 1 - jj, None)
        return jj + 1, jnp.max(carry_ref[...])

    lax.while_loop(more, body, (jnp.int32(0), jnp.max(carry_ref[...])))
    o_ref[0] = acc_ref[...].astype(o_ref.dtype)


def _attention(qkv, *, blk=256):
    B, S, _ = qkv.shape
    whole = lambda part: pl.BlockSpec((1, S, SB_WIDTH), lambda b, i: (b, 0, part),
                                      pipeline_mode=pl.Buffered(1))
    return pl.pallas_call(
        functools.partial(_attn_kernel, blk=blk),
        grid=(B, S // blk),
        in_specs=[
            pl.BlockSpec((1, blk, SB_WIDTH), lambda b, i: (b, i, 0)),
            whole(1), whole(2),
        ],
        out_specs=pl.BlockSpec((1, blk, SB_WIDTH), lambda b, i: (b, i, 0)),
        out_shape=jax.ShapeDtypeStruct((B, S, SB_WIDTH), BF16),
        scratch_shapes=[pltpu.VMEM((blk, SB_WIDTH), F32),
                        pltpu.VMEM((SB_HEADS * blk, 1), F32)],
        compiler_params=pltpu.CompilerParams(
            dimension_semantics=("arbitrary", "arbitrary"), vmem_limit_bytes=VMEM_LIMIT),
        name="sb_attention",
    )(qkv, qkv, qkv)


def _mixer_tail_kernel(attn_ref, u_ref, uh_ref, g_ref, x_ref,
                       poolw_ref, poolb_ref, pools_ref, wba_ref, wbp_ref, wout_ref,
                       post1_ref, gate1_ref, pre2_ref, sc2_ref, sh2_ref, rw_ref, rb_ref,
                       x1_ref, h2_ref, meta_ref, counts_ref, carry_ref, *, tm, tps):
    i = pl.program_id(0)
    D = x_ref.shape[1]
    d_pool = u_ref.shape[1]
    gdim = d_pool // POOL_GROUPS
    seq_tile = i % tps

    u = u_ref[...]
    halo = jnp.where(seq_tile == 0, 0.0, uh_ref[...])
    s = jnp.concatenate([halo, u], axis=0)
    sums = []
    span = 1
    for w in POOL_WINDOWS:
        while span < w:
            s = s + pltpu.roll(s, span, 0)
            span *= 2
        sums.append(s[POOL_HALO:])
    pos1 = (seq_tile * tm + 1 + lax.broadcasted_iota(jnp.int32, (tm, 1), 0)).astype(F32)
    colp = lax.broadcasted_iota(jnp.int32, (1, d_pool), 1)
    mean = sums[-1] / jnp.minimum(pos1, float(POOL_WINDOWS[-1]))
    for gi in range(POOL_GROUPS - 2, -1, -1):
        mean = jnp.where(colp < (gi + 1) * gdim,
                         sums[gi] / jnp.minimum(pos1, float(POOL_WINDOWS[gi])), mean)
    mixed = mean - u
    yp = jnp.dot(mixed.astype(BF16), poolw_ref[...], preferred_element_type=F32)
    pool = ((yp + poolb_ref[...]) * pools_ref[...]).astype(BF16)

    pa = jnp.dot(attn_ref[...], wba_ref[...], preferred_element_type=F32)
    pp = jnp.dot(pool, wbp_ref[...], preferred_element_type=F32)
    merged = g_ref[:, 0:D].astype(F32) * pa + g_ref[:, D:2 * D].astype(F32) * pp
    y = jnp.dot(merged.astype(BF16), wout_ref[...], preferred_element_type=F32)
    x1 = x_ref[...] + gate1_ref[0] * (y * _rms(y) * post1_ref[...])
    x1_ref[...] = x1
    h2 = (x1 * _rms(x1) * pre2_ref[...]) * (1.0 + sc2_ref[0]) + sh2_ref[0]
    h2_ref[...] = h2

    logits = jnp.dot(h2, rw_ref[...], precision=lax.Precision.HIGHEST,
                     preferred_element_type=F32) + rb_ref[...]
    lane = lax.broadcasted_iota(jnp.int32, (tm, LANES), 1).astype(F32)
    work = logits
    vals, idxs, hots = [], [], []
    for _ in range(TOP_K):
        m = jnp.max(work, axis=-1, keepdims=True)
        idx = jnp.min(jnp.where(work == m, lane, float(LANES)), axis=-1, keepdims=True)
        hot = lane == idx
        work = jnp.where(hot, -jnp.inf, work)
        vals.append(m)
        idxs.append(idx)
        hots.append(hot)
    exps = [jnp.exp(vk - vals[0]) for vk in vals]
    denom = exps[0] + exps[1] + exps[2] + exps[3]
    chosen = (hots[0] | hots[1] | hots[2] | hots[3]).astype(F32)

    @pl.when(i == 0)
    def _():
        carry_ref[...] = jnp.zeros_like(carry_ref)

    trow = lax.broadcasted_iota(jnp.int32, (tm, tm), 0)
    tcol = lax.broadcasted_iota(jnp.int32, (tm, tm), 1)
    before = (tcol < trow).astype(BF16)
    rank_all = jnp.dot(before, chosen.astype(BF16), preferred_element_type=F32) + carry_ref[...]
    meta = jnp.zeros((tm, LANES), F32)
    for kk in range(TOP_K):
        rk = jnp.sum(jnp.where(hots[kk], rank_all, 0.0), axis=-1, keepdims=True)
        meta = jnp.where(lane == float(META_E + kk), idxs[kk], meta)
        meta = jnp.where(lane == float(META_G + kk), exps[kk] / denom, meta)
        meta = jnp.where(lane == float(META_R + kk), rk, meta)
    meta_ref[...] = meta
    carry_ref[...] += jnp.sum(chosen, axis=0, keepdims=True)
    counts_ref[...] = carry_ref[...]


def _mixer_tail(attn, u, g, x2d, poolw_bd, poolb, pools, wba, wbp, wout,
                post1, gate1, pre2, sc2, sh2, rw_pad, rb_pad, *, seq, tm=256):
    T, D = x2d.shape
    d_pool = u.shape[1]
    tps = seq // tm
    hpt = tm // POOL_HALO

    def const(shape):
        return pl.BlockSpec(shape, lambda i: (0,) * len(shape))

    row = lambda w: pl.BlockSpec((tm, w), lambda i: (i, 0))
    vec = pl.BlockSpec((1, 1, D), lambda i: (i // tps, 0, 0))
    return pl.pallas_call(
        functools.partial(_mixer_tail_kernel, tm=tm, tps=tps),
        grid=(T // tm,),
        in_specs=[
            row(SB_WIDTH), row(d_pool),
            pl.BlockSpec((POOL_HALO, d_pool), lambda i: (jnp.maximum(i * hpt - 1, 0), 0)),
            row(2 * D), row(D),
            const((d_pool, d_pool)), const((1, d_pool)), const((1, d_pool)),
            const((SB_WIDTH, D)), const((d_pool, D)), const((D, D)),
            const((1, D)), vec, const((1, D)), vec, vec,
            const((D, LANES)), const((1, LANES)),
        ],
        out_specs=[row(D), row(D), row(LANES), const((1, LANES))],
        out_shape=[
            jax.ShapeDtypeStruct((T, D), F32),
            jax.ShapeDtypeStruct((T, D), F32),
            jax.ShapeDtypeStruct((T, LANES), F32),
            jax.ShapeDtypeStruct((1, LANES), F32),
        ],
        scratch_shapes=[pltpu.VMEM((1, LANES), F32)],
        compiler_params=pltpu.CompilerParams(
            dimension_semantics=("arbitrary",), vmem_limit_bytes=VMEM_LIMIT),
        name="mixer_tail",
    )(attn, u, u, g, x2d, poolw_bd, poolb, pools, wba, wbp, wout,
      post1, gate1, pre2, sc2, sh2, rw_pad, rb_pad)


def _row_copy_all(n_items, per_item, make_copy, wait_like, chunk):
    def issue(t, c):
        for k in range(per_item):
            make_copy(t, k).start()
        return c

    lax.fori_loop(0, n_items, issue, 0, unroll=2)
    n_rows = n_items * per_item
    for _ in range(n_rows // chunk):
        wait_like(chunk).wait()
    if n_rows % chunk:
        wait_like(n_rows % chunk).wait()


def _dispatch_kernel(dest_ref, pad_ref, h_ref, xr_ref, zero_ref, sem, *, tm, ppt):
    i = pl.program_id(0)

    def make_copy(t, k):
        return pltpu.make_async_copy(
            h_ref.at[pl.ds(t, 1), :], xr_ref.at[pl.ds(dest_ref[t * TOP_K + k], 1), :], sem)

    def wait_like(n):
        return pltpu.make_async_copy(h_ref.at[pl.ds(0, n), :], xr_ref.at[pl.ds(0, n), :], sem)

    _row_copy_all(tm, TOP_K, make_copy, wait_like, tm)

    zero_ref[...] = jnp.zeros_like(zero_ref)

    def make_pad_copy(j, _):
        return pltpu.make_async_copy(
            zero_ref.at[pl.ds(0, 1), :], xr_ref.at[pl.ds(pad_ref[i * ppt + j], 1), :], sem)

    _row_copy_all(ppt, 1, make_pad_copy, wait_like, tm)


def _dispatch(dest, pad_rows, h2, n_rows, *, tm=256):
    T, D = h2.shape
    steps = T // tm
    ppt = pad_rows.shape[0] // steps
    assert ppt * steps == pad_rows.shape[0]
    return pl.pallas_call(
        functools.partial(_dispatch_kernel, tm=tm, ppt=ppt),
        grid=(steps,),
        in_specs=[
            pl.BlockSpec((tm * TOP_K,), lambda i: (i,), memory_space=pltpu.SMEM),
            pl.BlockSpec(memory_space=pltpu.SMEM),
            pl.BlockSpec((tm, D), lambda i: (i, 0)),
        ],
        out_specs=pl.BlockSpec(memory_space=pl.ANY),
        out_shape=jax.ShapeDtypeStruct((n_rows, D), F32),
        scratch_shapes=[pltpu.VMEM((8, D), F32), pltpu.SemaphoreType.DMA],
        compiler_params=pltpu.CompilerParams(
            dimension_semantics=("arbitrary",), vmem_limit_bytes=VMEM_LIMIT),
        name="moe_dispatch",
    )(dest, pad_rows, h2)


def _ffn_kernel(be_ref, nused_ref, x_ref, wgu_ref, bgu_ref, wdn_ref, bdn_ref,
                o_ref, wgu_bf, wdn_bf):
    b = pl.program_id(0)
    de = wdn_ref.shape[2]

    @pl.when(b >= nused_ref[0])
    def _():
        o_ref[...] = jnp.zeros_like(o_ref)

    @pl.when(b < nused_ref[0])
    def _():
        prev = be_ref[jnp.maximum(b - 1, 0)]

        @pl.when((b == 0) | (be_ref[b] != prev))
        def _():
            wgu_bf[...] = wgu_ref[0, 0].astype(BF16)
            wdn_bf[...] = wdn_ref[0, 0].astype(BF16)

        x = x_ref[...].astype(BF16)
        gu = jnp.dot(x, wgu_bf[...], preferred_element_type=F32) + bgu_ref[0, 0]
        gate = jnp.minimum(gu[:, 0:de], SWIGLU_LIMIT)
        up = jnp.clip(gu[:, de:2 * de], -SWIGLU_LIMIT, SWIGLU_LIMIT)
        act = (up + 1.0) * gate * jax.nn.sigmoid(SWIGLU_ALPHA * gate)
        o_ref[...] = jnp.dot(act.astype(BF16), wdn_bf[...],
                             preferred_element_type=F32) + bdn_ref[0, 0]


def _ffn(block_e, n_used, xr, w_gu, b_gu, w_dn, b_dn, layer):
    n_rows, D = xr.shape
    L, E, _, n_gu = w_gu.shape
    de = w_dn.shape[2]
    n_blocks = n_rows // ROW_BLOCK

    def used(b, nu):
        return jnp.minimum(b, nu[0] - 1)

    grid_spec = pltpu.PrefetchScalarGridSpec(
        num_scalar_prefetch=2,
        grid=(n_blocks,),
        in_specs=[
            pl.BlockSpec((ROW_BLOCK, D), lambda b, be, nu: (used(b, nu), 0)),
            pl.BlockSpec((1, 1, D, n_gu), lambda b, be, nu: (layer, be[used(b, nu)], 0, 0)),
            pl.BlockSpec((1, 1, 1, n_gu), lambda b, be, nu: (layer, be[used(b, nu)], 0, 0)),
            pl.BlockSpec((1, 1, de, D), lambda b, be, nu: (layer, be[used(b, nu)], 0, 0)),
            pl.BlockSpec((1, 1, 1, D), lambda b, be, nu: (layer, be[used(b, nu)], 0, 0)),
        ],
        out_specs=pl.BlockSpec((ROW_BLOCK, D), lambda b, be, nu: (b, 0)),
        scratch_shapes=[pltpu.VMEM((D, n_gu), BF16), pltpu.VMEM((de, D), BF16)],
    )
    return pl.pallas_call(
        _ffn_kernel,
        grid_spec=grid_spec,
        out_shape=jax.ShapeDtypeStruct((n_rows, D), F32),
        compiler_params=pltpu.CompilerParams(
            dimension_semantics=("arbitrary",), vmem_limit_bytes=VMEM_LIMIT),
        name="moe_ffn",
    )(block_e, n_used, xr, w_gu, b_gu.reshape(L, E, 1, n_gu), w_dn, b_dn.reshape(L, E, 1, D))


def _combine_kernel(dest_ref, meta_ref, x1_ref, post2_ref, gate2_ref, yr_ref, o_ref, buf, sem, *, tm):
    def make_copy(t, k):
        return pltpu.make_async_copy(
            yr_ref.at[pl.ds(dest_ref[t * TOP_K + k], 1), :], buf.at[k, pl.ds(t, 1), :], sem)

    def wait_like(n):
        return pltpu.make_async_copy(yr_ref.at[pl.ds(0, n), :], buf.at[0, pl.ds(0, n), :], sem)

    _row_copy_all(tm, TOP_K, make_copy, wait_like, tm)
    y = meta_ref[:, META_G:META_G + 1] * buf[0]
    for kk in range(1, TOP_K):
        y = y + meta_ref[:, META_G + kk:META_G + kk + 1] * buf[kk]
    o_ref[...] = x1_ref[...] + gate2_ref[0] * (y * _rms(y) * post2_ref[...])


def _combine(dest, meta, x1, post2, gate2, yr, *, seq, tm=256):
    T, D = x1.shape
    tps = seq // tm
    return pl.pallas_call(
        functools.partial(_combine_kernel, tm=tm),
        grid=(T // tm,),
        in_specs=[
            pl.BlockSpec((tm * TOP_K,), lambda i: (i,), memory_space=pltpu.SMEM),
            pl.BlockSpec((tm, LANES), lambda i: (i, 0)),
            pl.BlockSpec((tm, D), lambda i: (i, 0)),
            pl.BlockSpec((1, D), lambda i: (0, 0)),
            pl.BlockSpec((1, 1, D), lambda i: (i // tps, 0, 0)),
            pl.BlockSpec(memory_space=pl.ANY),
        ],
        out_specs=pl.BlockSpec((tm, D), lambda i: (i, 0)),
        out_shape=jax.ShapeDtypeStruct((T, D), F32),
        scratch_shapes=[pltpu.VMEM((TOP_K, tm, D), F32), pltpu.SemaphoreType.DMA],
        compiler_params=pltpu.CompilerParams(
            dimension_semantics=("arbitrary",), vmem_limit_bytes=VMEM_LIMIT),
        name="moe_combine",
    )(dest, meta, x1, post2, gate2, yr)


def _block_diag(w):
    g, a, b = w.shape
    out = jnp.zeros((g * a, g * b), w.dtype)
    for i in range(g):
        out = out.at[i * a:(i + 1) * a, i * b:(i + 1) * b].set(w[i])
    return out


def kernel(x, c, ada_w, ada_b, pre1_g, post1_g, pre2_g, post2_g, w_in, pool_w, pool_b, pool_scale,
           w_br_attn, w_br_pool, w_out, router_w, router_b, w_gu, b_gu, w_dn, b_dn):
    B, S, D = x.shape
    L = ada_w.shape[0]
    T = B * S
    d_pool = D - SB_WIDTH
    n_rows = T * TOP_K + N_EXPERTS * ROW_BLOCK
    n_blocks = n_rows // ROW_BLOCK

    c_pad = jnp.pad(c, ((0, 8 - B), (0, 0)))
    mod = _ada(c_pad, ada_w, ada_b)[:, :B]

    x2d = x.reshape(T, D)
    for l in range(L):
        shift1, scale1, gate1, shift2, scale2, gate2 = [
            mod[l, :, i * D:(i + 1) * D].reshape(B, 1, D) for i in range(6)]

        qkv, u, g = _in_proj(x2d, pre1_g[l], scale1, shift1, w_in[l].astype(BF16), seq=S)
        attn = _attention(qkv.reshape(B, S, 3 * SB_WIDTH)).reshape(T, SB_WIDTH)

        rw_pad = jnp.pad(router_w[l], ((0, 0), (0, LANES - N_EXPERTS)))
        rb_pad = jnp.pad(router_b[l], (0, LANES - N_EXPERTS), constant_values=-jnp.inf)
        x1, h2, meta, counts = _mixer_tail(
            attn, u, g, x2d,
            _block_diag(pool_w[l]).astype(BF16), pool_b[l].reshape(1, d_pool),
            pool_scale[l].reshape(1, d_pool),
            w_br_attn[l].astype(BF16), w_br_pool[l].astype(BF16), w_out[l].astype(BF16),
            post1_g[l].reshape(1, D), gate1, pre2_g[l].reshape(1, D), scale2, shift2,
            rw_pad, rb_pad.reshape(1, LANES), seq=S)

        e_idx = meta[:, META_E:META_E + TOP_K].astype(jnp.int32)
        rank = meta[:, META_R:META_R + TOP_K].astype(jnp.int32)
        cnt = counts[0, :N_EXPERTS].astype(jnp.int32)
        padded = ((cnt + ROW_BLOCK - 1) // ROW_BLOCK) * ROW_BLOCK
        pend = jnp.cumsum(padded)
        pstart = pend - padded
        dest = (pstart[e_idx] + rank).reshape(-1)
        blk0 = jnp.arange(n_blocks, dtype=jnp.int32) * ROW_BLOCK
        block_e = jnp.minimum(jnp.sum(pend[None, :] <= blk0[:, None], axis=1), N_EXPERTS - 1).astype(jnp.int32)
        n_used = (pend[-1:] // ROW_BLOCK).astype(jnp.int32)
        gap = padded - cnt
        gap_end = jnp.cumsum(gap)
        p = jnp.arange(n_rows - T * TOP_K, dtype=jnp.int32)
        pe = jnp.sum(gap_end[None, :] <= p[:, None], axis=1)
        pe_c = jnp.minimum(pe, N_EXPERTS - 1)
        in_expert = pstart[pe_c] + cnt[pe_c] + (p - (gap_end - gap)[pe_c])
        pad_rows = jnp.where(pe < N_EXPERTS, in_expert, pend[-1] + (p - gap_end[-1])).astype(jnp.int32)

        xr = _dispatch(dest, pad_rows, h2, n_rows)
        yr = _ffn(block_e, n_used, xr, w_gu, b_gu, w_dn, b_dn, l)
        x2d = _combine(dest, meta, x1, post2_g[l].reshape(1, D), gate2, yr, seq=S)
    return x2d.reshape(B, S, D)
```

```python
import functools

import jax
import jax.numpy as jnp
from jax import lax
from jax.experimental import pallas as pl
from jax.experimental.pallas import tpu as pltpu

F32 = jnp.float32
BF16 = jnp.bfloat16

SB_HEADS = 4
SB_HEAD_DIM = 64
SB_WIDTH = SB_HEADS * SB_HEAD_DIM
POOL_WINDOWS = (2, 4, 8, 16)
POOL_GROUPS = len(POOL_WINDOWS)
POOL_HALO = 16
N_EXPERTS = 32
TOP_K = 4
ROW_BLOCK = 256
SWIGLU_LIMIT = 7.0
SWIGLU_ALPHA = 1.702
NORM_EPS = 1e-6
LOG2_E = 1.4426950408889634
F32_UNDERFLOW_LOG2 = -150.0

LANES = 128
VMEM_LIMIT = 52 * 1024 * 1024

META_E, META_G, META_R = 0, 4, 8


def _rms(x):
    return lax.rsqrt(jnp.mean(x * x, axis=-1, keepdims=True) + NORM_EPS)


def _ada_kernel(c_ref, w_ref, b_ref, o_ref):
    c = c_ref[...]
    ca = c * jax.nn.sigmoid(c)
    o_ref[0] = jnp.dot(ca, w_ref[0], precision=lax.Precision.HIGHEST,
                       preferred_element_type=F32) + b_ref[0]


def _ada(c_pad, ada_w, ada_b):
    L, D, N = ada_w.shape
    bp = c_pad.shape[0]
    tn = 1536
    return pl.pallas_call(
        _ada_kernel,
        grid=(L, N // tn),
        in_specs=[
            pl.BlockSpec((bp, D), lambda l, j: (0, 0)),
            pl.BlockSpec((1, D, tn), lambda l, j: (l, 0, j)),
            pl.BlockSpec((1, 1, tn), lambda l, j: (l, 0, j)),
        ],
        out_specs=pl.BlockSpec((1, bp, tn), lambda l, j: (l, 0, j)),
        out_shape=jax.ShapeDtypeStruct((L, bp, N), F32),
        compiler_params=pltpu.CompilerParams(
            dimension_semantics=("arbitrary", "arbitrary"), vmem_limit_bytes=VMEM_LIMIT),
        name="ada_mod",
    )(c_pad, ada_w, ada_b.reshape(L, 1, N))


def _in_proj_kernel(x_ref, g_ref, sc_ref, sh_ref, w_ref, qkv_ref, u_ref, gate_ref, *, d_pool):
    x = x_ref[...]
    h = (x * _rms(x) * g_ref[...]) * (1.0 + sc_ref[0]) + sh_ref[0]
    hb = h.astype(BF16)
    n_qkv = 3 * SB_WIDTH
    qkv = jnp.dot(hb, w_ref[:, 0:n_qkv], preferred_element_type=F32)
    col = lax.broadcasted_iota(jnp.int32, (1, n_qkv), 1)
    qkv = jnp.where(col < SB_WIDTH, qkv * (LOG2_E * SB_HEAD_DIM ** -0.5), qkv)
    qkv_ref[...] = qkv.astype(BF16)
    u_ref[...] = jnp.dot(hb, w_ref[:, n_qkv:n_qkv + d_pool], preferred_element_type=F32)
    gl = jnp.dot(hb, w_ref[:, n_qkv + d_pool:], preferred_element_type=F32)
    gate_ref[...] = jax.nn.sigmoid(gl).astype(BF16)


def _in_proj(x2d, g, scale, shift, w_bf, *, seq, tm=512):
    T, D = x2d.shape
    n_in = w_bf.shape[1]
    n_qkv = 3 * SB_WIDTH
    d_pool = D - SB_WIDTH
    n_gate = n_in - n_qkv - d_pool
    tps = seq // tm
    vec = pl.BlockSpec((1, 1, D), lambda i: (i // tps, 0, 0))
    return pl.pallas_call(
        functools.partial(_in_proj_kernel, d_pool=d_pool),
        grid=(T // tm,),
        in_specs=[
            pl.BlockSpec((tm, D), lambda i: (i, 0)),
            pl.BlockSpec((1, D), lambda i: (0, 0)),
            vec, vec,
            pl.BlockSpec((D, n_in), lambda i: (0, 0)),
        ],
        out_specs=[
            pl.BlockSpec((tm, n_qkv), lambda i: (i, 0)),
            pl.BlockSpec((tm, d_pool), lambda i: (i, 0)),
            pl.BlockSpec((tm, n_gate), lambda i: (i, 0)),
        ],
        out_shape=[
            jax.ShapeDtypeStruct((T, n_qkv), BF16),
            jax.ShapeDtypeStruct((T, d_pool), F32),
            jax.ShapeDtypeStruct((T, n_gate), BF16),
        ],
        compiler_params=pltpu.CompilerParams(
            dimension_semantics=("arbitrary",), vmem_limit_bytes=VMEM_LIMIT),
        name="prenorm_in_proj",
    )(x2d, g.reshape(1, D), scale, shift, w_bf)


def _attn_kernel(q_ref, k_ref, v_ref, o_ref, acc_ref, carry_ref, *, blk):
    i = pl.program_id(1)
    width = q_ref.shape[2]
    heads = width // SB_HEAD_DIM
    head_of_lane = lax.broadcasted_iota(jnp.int32, (1, width), 1) // SB_HEAD_DIM
    q = q_ref[0]
    zero = jnp.zeros((), BF16)
    qs = jnp.concatenate([jnp.where(head_of_lane == h, q, zero) for h in range(heads)], axis=0)
    row = lax.broadcasted_iota(jnp.int32, (blk, blk), 0)
    col = lax.broadcasted_iota(jnp.int32, (blk, blk), 1)
    tri = (row > col).astype(BF16)
    causal = jnp.concatenate([col < row] * heads, axis=0)

    def step(j, mask):
        ks = pl.multiple_of(j * blk, blk)
        k = k_ref[0, pl.ds(ks, blk), :]
        v = v_ref[0, pl.ds(ks, blk), :]
        t = lax.dot_general(qs, k, (((1,), (1,)), ((), ())), preferred_element_type=F32)
        soft = jnp.log2(1.0 + jnp.exp2(-jnp.abs(t)))
        log_take = jnp.minimum(t, 0.0) - soft
        log_keep = log_take - t
        if mask is not None:
            log_keep = jnp.where(mask, log_keep, 0.0)
        later = jnp.dot(log_keep.astype(BF16), tri, preferred_element_type=F32)
        a = jnp.exp2(log_take + later + carry_ref[...])
        if mask is not None:
            a = jnp.where(mask, a, 0.0)
        a = a.astype(BF16)
        a_wide = jnp.concatenate([a[h * blk:(h + 1) * blk] for h in range(heads)], axis=1)
        v_heads = jnp.concatenate([jnp.where(head_of_lane == h, v, zero) for h in range(heads)], axis=0)
        acc_ref[...] += jnp.dot(a_wide, v_heads, preferred_element_type=F32)
        carry_ref[...] += jnp.sum(log_keep, axis=-1, keepdims=True)

    acc_ref[...] = jnp.zeros_like(acc_ref)
    carry_ref[...] = jnp.zeros_like(carry_ref)
    step(i, causal)

    def more(state):
        jj, carry_max = state
        return jnp.logical_and(jj < i, carry_max > F32_UNDERFLOW_LOG2)

    def body(state):
        jj, _ = state
        step(i - 1 - jj, None)
        return jj + 1, jnp.max(carry_ref[...])

    lax.while_loop(more, body, (jnp.int32(0), jnp.max(carry_ref[...])))
    o_ref[0] = acc_ref[...].astype(o_ref.dtype)


def _attention(qkv, *, blk=256):
    B, S, _ = qkv.shape
    whole = lambda part: pl.BlockSpec((1, S, SB_WIDTH), lambda b, i: (b, 0, part),
                                      pipeline_mode=pl.Buffered(1))
    return pl.pallas_call(
        functools.partial(_attn_kernel, blk=blk),
        grid=(B, S // blk),
        in_specs=[
            pl.BlockSpec((1, blk, SB_WIDTH), lambda b, i: (b, i, 0)),
            whole(1), whole(2),
        ],
        out_specs=pl.BlockSpec((1, blk, SB_WIDTH), lambda b, i: (b, i, 0)),
        out_shape=jax.ShapeDtypeStruct((B, S, SB_WIDTH), BF16),
        scratch_shapes=[pltpu.VMEM((blk, SB_WIDTH), F32),
                        pltpu.VMEM((SB_HEADS * blk, 1), F32)],
        compiler_params=pltpu.CompilerParams(
            dimension_semantics=("arbitrary", "arbitrary"), vmem_limit_bytes=VMEM_LIMIT),
        name="sb_attention",
    )(qkv, qkv, qkv)


def _mixer_tail_kernel(attn_ref, u_ref, uh_ref, g_ref, x_ref,
                       poolw_ref, poolb_ref, pools_ref, wba_ref, wbp_ref, wout_ref,
                       post1_ref, gate1_ref, pre2_ref, sc2_ref, sh2_ref, rw_ref, rb_ref,
                       x1_ref, h2_ref, meta_ref, counts_ref, carry_ref, *, tm, tps):
    i = pl.program_id(0)
    D = x_ref.shape[1]
    d_pool = u_ref.shape[1]
    gdim = d_pool // POOL_GROUPS
    seq_tile = i % tps

    u = u_ref[...]
    halo = jnp.where(seq_tile == 0, 0.0, uh_ref[...])
    s = jnp.concatenate([halo, u], axis=0)
    sums = []
    span = 1
    for w in POOL_WINDOWS:
        while span < w:
            s = s + pltpu.roll(s, span, 0)
            span *= 2
        sums.append(s[POOL_HALO:])
    pos1 = (seq_tile * tm + 1 + lax.broadcasted_iota(jnp.int32, (tm, 1), 0)).astype(F32)
    colp = lax.broadcasted_iota(jnp.int32, (1, d_pool), 1)
    mean = sums[-1] / jnp.minimum(pos1, float(POOL_WINDOWS[-1]))
    for gi in range(POOL_GROUPS - 2, -1, -1):
        mean = jnp.where(colp < (gi + 1) * gdim,
                         sums[gi] / jnp.minimum(pos1, float(POOL_WINDOWS[gi])), mean)
    mixed = mean - u
    yp = jnp.dot(mixed.astype(BF16), poolw_ref[...], preferred_element_type=F32)
    pool = ((yp + poolb_ref[...]) * pools_ref[...]).astype(BF16)

    pa = jnp.dot(attn_ref[...], wba_ref[...], preferred_element_type=F32)
    pp = jnp.dot(pool, wbp_ref[...], preferred_element_type=F32)
    merged = g_ref[:, 0:D].astype(F32) * pa + g_ref[:, D:2 * D].astype(F32) * pp
    y = jnp.dot(merged.astype(BF16), wout_ref[...], preferred_element_type=F32)
    x1 = x_ref[...] + gate1_ref[0] * (y * _rms(y) * post1_ref[...])
    x1_ref[...] = x1
    h2 = (x1 * _rms(x1) * pre2_ref[...]) * (1.0 + sc2_ref[0]) + sh2_ref[0]
    h2_ref[...] = h2

    logits = jnp.dot(h2, rw_ref[...], precision=lax.Precision.HIGHEST,
                     preferred_element_type=F32) + rb_ref[...]
    lane = lax.broadcasted_iota(jnp.int32, (tm, LANES), 1).astype(F32)
    work = logits
    vals, idxs, hots = [], [], []
    for _ in range(TOP_K):
        m = jnp.max(work, axis=-1, keepdims=True)
        idx = jnp.min(jnp.where(work == m, lane, float(LANES)), axis=-1, keepdims=True)
        hot = lane == idx
        work = jnp.where(hot, -jnp.inf, work)
        vals.append(m)
        idxs.append(idx)
        hots.append(hot)
    exps = [jnp.exp(vk - vals[0]) for vk in vals]
    denom = exps[0] + exps[1] + exps[2] + exps[3]
    chosen = (hots[0] | hots[1] | hots[2] | hots[3]).astype(F32)

    @pl.when(i == 0)
    def _():
        carry_ref[...] = jnp.zeros_like(carry_ref)

    trow = lax.broadcasted_iota(jnp.int32, (tm, tm), 0)
    tcol = lax.broadcasted_iota(jnp.int32, (tm, tm), 1)
    before = (tcol < trow).astype(BF16)
    rank_all = jnp.dot(before, chosen.astype(BF16), preferred_element_type=F32) + carry_ref[...]
    meta = jnp.zeros((tm, LANES), F32)
    for kk in range(TOP_K):
        rk = jnp.sum(jnp.where(hots[kk], rank_all, 0.0), axis=-1, keepdims=True)
        meta = jnp.where(lane == float(META_E + kk), idxs[kk], meta)
        meta = jnp.where(lane == float(META_G + kk), exps[kk] / denom, meta)
        meta = jnp.where(lane == float(META_R + kk), rk, meta)
    meta_ref[...] = meta
    carry_ref[...] += jnp.sum(chosen, axis=0, keepdims=True)
    counts_ref[...] = carry_ref[...]


def _mixer_tail(attn, u, g, x2d, poolw_bd, poolb, pools, wba, wbp, wout,
                post1, gate1, pre2, sc2, sh2, rw_pad, rb_pad, *, seq, tm=512):
    T, D = x2d.shape
    d_pool = u.shape[1]
    tps = seq // tm
    hpt = tm // POOL_HALO

    def const(shape):
        return pl.BlockSpec(shape, lambda i: (0,) * len(shape))

    row = lambda w: pl.BlockSpec((tm, w), lambda i: (i, 0))
    vec = pl.BlockSpec((1, 1, D), lambda i: (i // tps, 0, 0))
    return pl.pallas_call(
        functools.partial(_mixer_tail_kernel, tm=tm, tps=tps),
        grid=(T // tm,),
        in_specs=[
            row(SB_WIDTH), row(d_pool),
            pl.BlockSpec((POOL_HALO, d_pool), lambda i: (jnp.maximum(i * hpt - 1, 0), 0)),
            row(2 * D), row(D),
            const((d_pool, d_pool)), const((1, d_pool)), const((1, d_pool)),
            const((SB_WIDTH, D)), const((d_pool, D)), const((D, D)),
            const((1, D)), vec, const((1, D)), vec, vec,
            const((D, LANES)), const((1, LANES)),
        ],
        out_specs=[row(D), row(D), row(LANES), const((1, LANES))],
        out_shape=[
            jax.ShapeDtypeStruct((T, D), F32),
            jax.ShapeDtypeStruct((T, D), F32),
            jax.ShapeDtypeStruct((T, LANES), F32),
            jax.ShapeDtypeStruct((1, LANES), F32),
        ],
        scratch_shapes=[pltpu.VMEM((1, LANES), F32)],
        compiler_params=pltpu.CompilerParams(
            dimension_semantics=("arbitrary",), vmem_limit_bytes=VMEM_LIMIT),
        name="mixer_tail",
    )(attn, u, u, g, x2d, poolw_bd, poolb, pools, wba, wbp, wout,
      post1, gate1, pre2, sc2, sh2, rw_pad, rb_pad)


def _row_map_kernel(dest_ref, pad_ref, inv_ref, *, chunk, n_pairs):
    i = pl.program_id(0)

    @pl.when(i == 0)
    def _():
        def pad(p, c):
            inv_ref[pad_ref[p]] = n_pairs + p
            return c

        lax.fori_loop(0, pad_ref.shape[0], pad, 0, unroll=8)

    def real(j, c):
        inv_ref[dest_ref[j]] = i * chunk + j
        return c

    lax.fori_loop(0, chunk, real, 0, unroll=8)


def _row_map(dest, pad_rows, *, chunk=1024):
    n_pairs = dest.shape[0]
    n_rows = n_pairs + pad_rows.shape[0]
    return pl.pallas_call(
        functools.partial(_row_map_kernel, chunk=chunk, n_pairs=n_pairs),
        grid=(n_pairs // chunk,),
        in_specs=[
            pl.BlockSpec((chunk,), lambda i: (i,), memory_space=pltpu.SMEM),
            pl.BlockSpec(memory_space=pltpu.SMEM),
        ],
        out_specs=pl.BlockSpec(memory_space=pltpu.SMEM),
        out_shape=jax.ShapeDtypeStruct((n_rows,), jnp.int32),
        compiler_params=pltpu.CompilerParams(dimension_semantics=("arbitrary",)),
        name="moe_row_map",
    )(dest, pad_rows)


def _ffn_kernel(be_ref, prime_ref, prev_ref, cur_ref, next_ref, h_ref,
                wgu_ref, bgu_ref, wdn_ref, bdn_ref, yg_ref,
                wgu_bf, wdn_bf, xbuf0, xbuf1, stage0, stage1, sem_g, sem_s, *, n_tok, n_blocks):
    b = pl.program_id(0)
    xbuf = (xbuf0, xbuf1)
    stage = (stage0, stage1)
    rb = xbuf0.shape[0]
    de = wdn_ref.shape[2]

    def gather(pairs_ref, s, r):
        tok = jnp.minimum(lax.shift_right_logical(pairs_ref[0, 0, r], 2), n_tok - 1)
        return pltpu.make_async_copy(
            h_ref.at[pl.ds(tok, 1), :], xbuf[s].at[pl.ds(r, 1), :], sem_g.at[s])

    def scatter(pairs_ref, s, r):
        return pltpu.make_async_copy(
            stage[s].at[pl.ds(r, 1), :], yg_ref.at[pl.ds(pairs_ref[0, 0, r], 1), :], sem_s.at[s])

    def gathered(s):
        return pltpu.make_async_copy(h_ref.at[pl.ds(0, rb), :], xbuf[s], sem_g.at[s])

    def scattered(s):
        return pltpu.make_async_copy(stage[s], yg_ref.at[pl.ds(0, rb), :], sem_s.at[s])

    def start_all(make, pairs_ref, s):
        def one(r, c):
            make(pairs_ref, s, r).start()
            return c

        lax.fori_loop(0, rb, one, 0, unroll=8)

    @pl.when(b == 0)
    def _():
        stage0[...] = jnp.zeros_like(stage0)
        stage1[...] = jnp.zeros_like(stage1)
        start_all(gather, cur_ref, 0)
        start_all(scatter, prime_ref, 0)

    @pl.when((b == 0) | (be_ref[b] != be_ref[jnp.maximum(b - 1, 0)]))
    def _():
        wgu_bf[...] = wgu_ref[0, 0].astype(BF16)
        wdn_bf[...] = wdn_ref[0, 0].astype(BF16)

    def block_step(slot):
        other = 1 - slot
        gathered(slot).wait()
        for r in range(rb):
            gather(next_ref, other, r).start()
        x = xbuf[slot][...].astype(BF16)
        gu = jnp.dot(x, wgu_bf[...], preferred_element_type=F32) + bgu_ref[0, 0]
        for r in range(rb):
            scatter(prev_ref, other, r).start()
        gate = jnp.minimum(gu[:, 0:de], SWIGLU_LIMIT)
        up = jnp.clip(gu[:, de:2 * de], -SWIGLU_LIMIT, SWIGLU_LIMIT)
        act = (up + 1.0) * gate * jax.nn.sigmoid(SWIGLU_ALPHA * gate)
        y = jnp.dot(act.astype(BF16), wdn_bf[...], preferred_element_type=F32) + bdn_ref[0, 0]
        scattered(slot).wait()
        stage[slot][...] = y

    for parity in range(2):
        pl.when(b % 2 == parity)(functools.partial(block_step, parity))

    @pl.when(b == n_blocks - 1)
    def _():
        slot = (n_blocks - 1) % 2
        start_all(scatter, cur_ref, slot)
        scattered(slot).wait()
        scattered(1 - slot).wait()
        gathered(1 - slot).wait()


def _ffn(block_e, pairs, h2, w_gu, b_gu, w_dn, b_dn, layer):
    T, D = h2.shape
    L, E, _, n_gu = w_gu.shape
    de = w_dn.shape[2]
    n_blocks = pairs.shape[0] - 2
    n_out = (n_blocks + 2) * ROW_BLOCK

    def pair_view(index):
        return pl.BlockSpec((1, 1, ROW_BLOCK), lambda b, be: (index(b), 0, 0),
                            memory_space=pltpu.SMEM)

    grid_spec = pltpu.PrefetchScalarGridSpec(
        num_scalar_prefetch=1,
        grid=(n_blocks,),
        in_specs=[
            pair_view(lambda b: 0),
            pair_view(lambda b: b + 1),
            pair_view(lambda b: b + 2),
            pair_view(lambda b: jnp.minimum(b + 1, n_blocks - 1) + 2),
            pl.BlockSpec(memory_space=pl.ANY),
            pl.BlockSpec((1, 1, D, n_gu), lambda b, be: (layer, be[b], 0, 0)),
            pl.BlockSpec((1, 1, 1, n_gu), lambda b, be: (layer, be[b], 0, 0)),
            pl.BlockSpec((1, 1, de, D), lambda b, be: (layer, be[b], 0, 0)),
            pl.BlockSpec((1, 1, 1, D), lambda b, be: (layer, be[b], 0, 0)),
        ],
        out_specs=pl.BlockSpec(memory_space=pl.ANY),
        scratch_shapes=[
            pltpu.VMEM((D, n_gu), BF16), pltpu.VMEM((de, D), BF16),
            pltpu.VMEM((ROW_BLOCK, D), F32), pltpu.VMEM((ROW_BLOCK, D), F32),
            pltpu.VMEM((ROW_BLOCK, D), F32), pltpu.VMEM((ROW_BLOCK, D), F32),
            pltpu.SemaphoreType.DMA((2,)), pltpu.SemaphoreType.DMA((2,)),
        ],
    )
    return pl.pallas_call(
        functools.partial(_ffn_kernel, n_tok=T, n_blocks=n_blocks),
        grid_spec=grid_spec,
        out_shape=jax.ShapeDtypeStruct((n_out, D), F32),
        compiler_params=pltpu.CompilerParams(
            dimension_semantics=("arbitrary",), vmem_limit_bytes=VMEM_LIMIT),
        name="moe_ffn",
    )(block_e, pairs, pairs, pairs, pairs, h2,
      w_gu, b_gu.reshape(L, E, 1, n_gu), w_dn, b_dn.reshape(L, E, 1, D))


def _combine_kernel(yg_ref, meta_ref, x1_ref, post2_ref, gate2_ref, o_ref):
    D = x1_ref.shape[1]
    y = meta_ref[:, META_G:META_G + 1] * yg_ref[:, 0:D]
    for kk in range(1, TOP_K):
        y = y + meta_ref[:, META_G + kk:META_G + kk + 1] * yg_ref[:, kk * D:(kk + 1) * D]
    o_ref[...] = x1_ref[...] + gate2_ref[0] * (y * _rms(y) * post2_ref[...])


def _combine(yg_wide, meta, x1, post2, gate2, *, seq, tm=256):
    T, D = x1.shape
    tps = seq // tm
    return pl.pallas_call(
        _combine_kernel,
        grid=(T // tm,),
        in_specs=[
            pl.BlockSpec((tm, TOP_K * D), lambda i: (i, 0)),
            pl.BlockSpec((tm, LANES), lambda i: (i, 0)),
            pl.BlockSpec((tm, D), lambda i: (i, 0)),
            pl.BlockSpec((1, D), lambda i: (0, 0)),
            pl.BlockSpec((1, 1, D), lambda i: (i // tps, 0, 0)),
        ],
        out_specs=pl.BlockSpec((tm, D), lambda i: (i, 0)),
        out_shape=jax.ShapeDtypeStruct((T, D), F32),
        compiler_params=pltpu.CompilerParams(
            dimension_semantics=("arbitrary",), vmem_limit_bytes=VMEM_LIMIT),
        name="moe_combine",
    )(yg_wide, meta, x1, post2, gate2)


def _block_diag(w):
    g, a, b = w.shape
    out = jnp.zeros((g * a, g * b), w.dtype)
    for i in range(g):
        out = out.at[i * a:(i + 1) * a, i * b:(i + 1) * b].set(w[i])
    return out


def kernel(x, c, ada_w, ada_b, pre1_g, post1_g, pre2_g, post2_g, w_in, pool_w, pool_b, pool_scale,
           w_br_attn, w_br_pool, w_out, router_w, router_b, w_gu, b_gu, w_dn, b_dn):
    B, S, D = x.shape
    L = ada_w.shape[0]
    T = B * S
    d_pool = D - SB_WIDTH
    n_rows = T * TOP_K + N_EXPERTS * ROW_BLOCK
    n_blocks = n_rows // ROW_BLOCK

    c_pad = jnp.pad(c, ((0, 8 - B), (0, 0)))
    mod = _ada(c_pad, ada_w, ada_b)[:, :B]

    x2d = x.reshape(T, D)
    for l in range(L):
        shift1, scale1, gate1, shift2, scale2, gate2 = [
            mod[l, :, i * D:(i + 1) * D].reshape(B, 1, D) for i in range(6)]

        qkv, u, g = _in_proj(x2d, pre1_g[l], scale1, shift1, w_in[l].astype(BF16), seq=S)
        attn = _attention(qkv.reshape(B, S, 3 * SB_WIDTH)).reshape(T, SB_WIDTH)

        rw_pad = jnp.pad(router_w[l], ((0, 0), (0, LANES - N_EXPERTS)))
        rb_pad = jnp.pad(router_b[l], (0, LANES - N_EXPERTS), constant_values=-jnp.inf)
        x1, h2, meta, counts = _mixer_tail(
            attn, u, g, x2d,
            _block_diag(pool_w[l]).astype(BF16), pool_b[l].reshape(1, d_pool),
            pool_scale[l].reshape(1, d_pool),
            w_br_attn[l].astype(BF16), w_br_pool[l].astype(BF16), w_out[l].astype(BF16),
            post1_g[l].reshape(1, D), gate1, pre2_g[l].reshape(1, D), scale2, shift2,
            rw_pad, rb_pad.reshape(1, LANES), seq=S)

        e_idx = meta[:, META_E:META_E + TOP_K].astype(jnp.int32)
        rank = meta[:, META_R:META_R + TOP_K].astype(jnp.int32)
        cnt = counts[0, :N_EXPERTS].astype(jnp.int32)
        padded = ((cnt + ROW_BLOCK - 1) // ROW_BLOCK) * ROW_BLOCK
        pend = jnp.cumsum(padded)
        pstart = pend - padded
        dest = (pstart[e_idx] + rank).reshape(-1)
        blk0 = jnp.arange(n_blocks, dtype=jnp.int32) * ROW_BLOCK
        block_e = jnp.minimum(jnp.sum(pend[None, :] <= blk0[:, None], axis=1), N_EXPERTS - 1).astype(jnp.int32)
        gap = padded - cnt
        gap_end = jnp.cumsum(gap)
        p = jnp.arange(n_rows - T * TOP_K, dtype=jnp.int32)
        pe = jnp.sum(gap_end[None, :] <= p[:, None], axis=1)
        pe_c = jnp.minimum(pe, N_EXPERTS - 1)
        in_expert = pstart[pe_c] + cnt[pe_c] + (p - (gap_end - gap)[pe_c])
        pad_rows = jnp.where(pe < N_EXPERTS, in_expert, pend[-1] + (p - gap_end[-1])).astype(jnp.int32)

        spare = n_rows + jnp.arange(2 * ROW_BLOCK, dtype=jnp.int32)
        pairs = jnp.concatenate([spare, _row_map(dest, pad_rows)]).reshape(n_blocks + 2, 1, ROW_BLOCK)
        yg = _ffn(block_e, pairs, h2, w_gu, b_gu, w_dn, b_dn, l)
        x2d = _combine(yg.reshape(-1, TOP_K * D), meta, x1, post2_g[l].reshape(1, D), gate2, seq=S)
    return x2d.reshape(B, S, D)
```

```python
import functools

import jax
import jax.numpy as jnp
from jax import lax
from jax.experimental import pallas as pl
from jax.experimental.pallas import tpu as pltpu

F32 = jnp.float32
BF16 = jnp.bfloat16

SB_HEADS = 4
SB_HEAD_DIM = 64
SB_WIDTH = SB_HEADS * SB_HEAD_DIM
POOL_WINDOWS = (2, 4, 8, 16)
POOL_GROUPS = len(POOL_WINDOWS)
POOL_HALO = 16
N_EXPERTS = 32
TOP_K = 4
ROW_BLOCK = 256
SWIGLU_LIMIT = 7.0
SWIGLU_ALPHA = 1.702
NORM_EPS = 1e-6
LOG2_E = 1.4426950408889634
F32_UNDERFLOW_LOG2 = -150.0

LANES = 128
VMEM_LIMIT = 52 * 1024 * 1024

META_E, META_G, META_R = 0, 4, 8


def _rms(x):
    return lax.rsqrt(jnp.mean(x * x, axis=-1, keepdims=True) + NORM_EPS)


def _ada_kernel(c_ref, w_ref, b_ref, o_ref):
    c = c_ref[...]
    ca = c * jax.nn.sigmoid(c)
    o_ref[0] = jnp.dot(ca, w_ref[0], precision=lax.Precision.HIGHEST,
                       preferred_element_type=F32) + b_ref[0]


def _ada(c_pad, ada_w, ada_b):
    L, D, N = ada_w.shape
    bp = c_pad.shape[0]
    tn = 1536
    return pl.pallas_call(
        _ada_kernel,
        grid=(L, N // tn),
        in_specs=[
            pl.BlockSpec((bp, D), lambda l, j: (0, 0)),
            pl.BlockSpec((1, D, tn), lambda l, j: (l, 0, j)),
            pl.BlockSpec((1, 1, tn), lambda l, j: (l, 0, j)),
        ],
        out_specs=pl.BlockSpec((1, bp, tn), lambda l, j: (l, 0, j)),
        out_shape=jax.ShapeDtypeStruct((L, bp, N), F32),
        compiler_params=pltpu.CompilerParams(
            dimension_semantics=("arbitrary", "arbitrary"), vmem_limit_bytes=VMEM_LIMIT),
        name="ada_mod",
    )(c_pad, ada_w, ada_b.reshape(L, 1, N))


def _in_proj_kernel(x_ref, g_ref, sc_ref, sh_ref, w_ref, qkv_ref, u_ref, gate_ref, *, d_pool):
    x = x_ref[...]
    h = (x * _rms(x) * g_ref[...]) * (1.0 + sc_ref[0]) + sh_ref[0]
    hb = h.astype(BF16)
    n_qkv = 3 * SB_WIDTH
    qkv = jnp.dot(hb, w_ref[:, 0:n_qkv], preferred_element_type=F32)
    col = lax.broadcasted_iota(jnp.int32, (1, n_qkv), 1)
    qkv = jnp.where(col < SB_WIDTH, qkv * (LOG2_E * SB_HEAD_DIM ** -0.5), qkv)
    qkv_ref[...] = qkv.astype(BF16)
    u_ref[...] = jnp.dot(hb, w_ref[:, n_qkv:n_qkv + d_pool], preferred_element_type=F32)
    gl = jnp.dot(hb, w_ref[:, n_qkv + d_pool:], preferred_element_type=F32)
    gate_ref[...] = jax.nn.sigmoid(gl).astype(BF16)


def _in_proj(x2d, g, scale, shift, w_bf, *, seq, tm=512):
    T, D = x2d.shape
    n_in = w_bf.shape[1]
    n_qkv = 3 * SB_WIDTH
    d_pool = D - SB_WIDTH
    n_gate = n_in - n_qkv - d_pool
    tps = seq // tm
    vec = pl.BlockSpec((1, 1, D), lambda i: (i // tps, 0, 0))
    return pl.pallas_call(
        functools.partial(_in_proj_kernel, d_pool=d_pool),
        grid=(T // tm,),
        in_specs=[
            pl.BlockSpec((tm, D), lambda i: (i, 0)),
            pl.BlockSpec((1, D), lambda i: (0, 0)),
            vec, vec,
            pl.BlockSpec((D, n_in), lambda i: (0, 0)),
        ],
        out_specs=[
            pl.BlockSpec((tm, n_qkv), lambda i: (i, 0)),
            pl.BlockSpec((tm, d_pool), lambda i: (i, 0)),
            pl.BlockSpec((tm, n_gate), lambda i: (i, 0)),
        ],
        out_shape=[
            jax.ShapeDtypeStruct((T, n_qkv), BF16),
            jax.ShapeDtypeStruct((T, d_pool), F32),
            jax.ShapeDtypeStruct((T, n_gate), BF16),
        ],
        compiler_params=pltpu.CompilerParams(
            dimension_semantics=("arbitrary",), vmem_limit_bytes=VMEM_LIMIT),
        name="prenorm_in_proj",
    )(x2d, g.reshape(1, D), scale, shift, w_bf)


def _attn_kernel(q_ref, k_ref, v_ref, o_ref, acc_ref, carry_ref, *, blk):
    i = pl.program_id(1)
    width = q_ref.shape[2]
    heads = width // SB_HEAD_DIM
    head_of_lane = lax.broadcasted_iota(jnp.int32, (1, width), 1) // SB_HEAD_DIM
    q = q_ref[0]
    zero = jnp.zeros((), BF16)
    qs = jnp.concatenate([jnp.where(head_of_lane == h, q, zero) for h in range(heads)], axis=0)
    row = lax.broadcasted_iota(jnp.int32, (blk, blk), 0)
    col = lax.broadcasted_iota(jnp.int32, (blk, blk), 1)
    tri = (row > col).astype(BF16)
    causal = jnp.concatenate([col < row] * heads, axis=0)

    def step(j, mask):
        ks = pl.multiple_of(j * blk, blk)
        k = k_ref[0, pl.ds(ks, blk), :]
        v = v_ref[0, pl.ds(ks, blk), :]
        t = lax.dot_general(qs, k, (((1,), (1,)), ((), ())), preferred_element_type=F32)
        soft = jnp.log2(1.0 + jnp.exp2(-jnp.abs(t)))
        log_take = jnp.minimum(t, 0.0) - soft
        log_keep = log_take - t
        if mask is not None:
            log_keep = jnp.where(mask, log_keep, 0.0)
        later = jnp.dot(log_keep.astype(BF16), tri, preferred_element_type=F32)
        a = jnp.exp2(log_take + later + carry_ref[...])
        if mask is not None:
            a = jnp.where(mask, a, 0.0)
        a = a.astype(BF16)
        a_wide = jnp.concatenate([a[h * blk:(h + 1) * blk] for h in range(heads)], axis=1)
        v_heads = jnp.concatenate([jnp.where(head_of_lane == h, v, zero) for h in range(heads)], axis=0)
        acc_ref[...] += jnp.dot(a_wide, v_heads, preferred_element_type=F32)
        carry_ref[...] += jnp.sum(log_keep, axis=-1, keepdims=True)

    acc_ref[...] = jnp.zeros_like(acc_ref)
    carry_ref[...] = jnp.zeros_like(carry_ref)
    step(i, causal)

    def more(state):
        jj, carry_max = state
        return jnp.logical_and(jj < i, carry_max > F32_UNDERFLOW_LOG2)

    def body(state):
        jj, _ = state
        step(i - 1 - jj, None)
        return jj + 1, jnp.max(carry_ref[...])

    lax.while_loop(more, body, (jnp.int32(0), jnp.max(carry_ref[...])))
    o_ref[0] = acc_ref[...].astype(o_ref.dtype)


def _attention(qkv, *, blk=256):
    B, S, _ = qkv.shape
    whole = lambda part: pl.BlockSpec((1, S, SB_WIDTH), lambda b, i: (b, 0, part),
                                      pipeline_mode=pl.Buffered(1))
    return pl.pallas_call(
        functools.partial(_attn_kernel, blk=blk),
        grid=(B, S // blk),
        in_specs=[
            pl.BlockSpec((1, blk, SB_WIDTH), lambda b, i: (b, i, 0)),
            whole(1), whole(2),
        ],
        out_specs=pl.BlockSpec((1, blk, SB_WIDTH), lambda b, i: (b, i, 0)),
        out_shape=jax.ShapeDtypeStruct((B, S, SB_WIDTH), BF16),
        scratch_shapes=[pltpu.VMEM((blk, SB_WIDTH), F32),
                        pltpu.VMEM((SB_HEADS * blk, 1), F32)],
        compiler_params=pltpu.CompilerParams(
            dimension_semantics=("arbitrary", "arbitrary"), vmem_limit_bytes=VMEM_LIMIT),
        name="sb_attention",
    )(qkv, qkv, qkv)


def _mixer_tail_kernel(attn_ref, u_ref, uh_ref, g_ref, x_ref,
                       poolw_ref, poolb_ref, pools_ref, wba_ref, wbp_ref, wout_ref,
                       post1_ref, gate1_ref, pre2_ref, sc2_ref, sh2_ref, rw_ref, rb_ref,
                       x1_ref, h2_ref, meta_ref, counts_ref, carry_ref, *, tm, tps):
    i = pl.program_id(0)
    D = x_ref.shape[1]
    d_pool = u_ref.shape[1]
    gdim = d_pool // POOL_GROUPS
    seq_tile = i % tps

    u = u_ref[...]
    halo = jnp.where(seq_tile == 0, 0.0, uh_ref[...])
    s = jnp.concatenate([halo, u], axis=0)
    sums = []
    span = 1
    for w in POOL_WINDOWS:
        while span < w:
            s = s + pltpu.roll(s, span, 0)
            span *= 2
        sums.append(s[POOL_HALO:])
    pos1 = (seq_tile * tm + 1 + lax.broadcasted_iota(jnp.int32, (tm, 1), 0)).astype(F32)
    colp = lax.broadcasted_iota(jnp.int32, (1, d_pool), 1)
    mean = sums[-1] / jnp.minimum(pos1, float(POOL_WINDOWS[-1]))
    for gi in range(POOL_GROUPS - 2, -1, -1):
        mean = jnp.where(colp < (gi + 1) * gdim,
                         sums[gi] / jnp.minimum(pos1, float(POOL_WINDOWS[gi])), mean)
    mixed = mean - u
    yp = jnp.dot(mixed.astype(BF16), poolw_ref[...], preferred_element_type=F32)
    pool = ((yp + poolb_ref[...]) * pools_ref[...]).astype(BF16)

    pa = jnp.dot(attn_ref[...], wba_ref[...], preferred_element_type=F32)
    pp = jnp.dot(pool, wbp_ref[...], preferred_element_type=F32)
    merged = g_ref[:, 0:D].astype(F32) * pa + g_ref[:, D:2 * D].astype(F32) * pp
    y = jnp.dot(merged.astype(BF16), wout_ref[...], preferred_element_type=F32)
    x1 = x_ref[...] + gate1_ref[0] * (y * _rms(y) * post1_ref[...])
    x1_ref[...] = x1
    h2 = (x1 * _rms(x1) * pre2_ref[...]) * (1.0 + sc2_ref[0]) + sh2_ref[0]
    h2_ref[...] = h2

    logits = jnp.dot(h2, rw_ref[...], precision=lax.Precision.HIGHEST,
                     preferred_element_type=F32) + rb_ref[...]
    lane = lax.broadcasted_iota(jnp.int32, (tm, LANES), 1).astype(F32)
    work = logits
    vals, idxs, hots = [], [], []
    for _ in range(TOP_K):
        m = jnp.max(work, axis=-1, keepdims=True)
        idx = jnp.min(jnp.where(work == m, lane, float(LANES)), axis=-1, keepdims=True)
        hot = lane == idx
        work = jnp.where(hot, -jnp.inf, work)
        vals.append(m)
        idxs.append(idx)
        hots.append(hot)
    exps = [jnp.exp(vk - vals[0]) for vk in vals]
    denom = exps[0] + exps[1] + exps[2] + exps[3]
    chosen = (hots[0] | hots[1] | hots[2] | hots[3]).astype(F32)

    @pl.when(i == 0)
    def _():
        carry_ref[...] = jnp.zeros_like(carry_ref)

    trow = lax.broadcasted_iota(jnp.int32, (tm, tm), 0)
    tcol = lax.broadcasted_iota(jnp.int32, (tm, tm), 1)
    before = (tcol < trow).astype(BF16)
    rank_all = jnp.dot(before, chosen.astype(BF16), preferred_element_type=F32) + carry_ref[...]
    meta = jnp.zeros((tm, LANES), F32)
    for kk in range(TOP_K):
        rk = jnp.sum(jnp.where(hots[kk], rank_all, 0.0), axis=-1, keepdims=True)
        meta = jnp.where(lane == float(META_E + kk), idxs[kk], meta)
        meta = jnp.where(lane == float(META_G + kk), exps[kk] / denom, meta)
        meta = jnp.where(lane == float(META_R + kk), rk, meta)
    meta_ref[...] = meta
    carry_ref[...] += jnp.sum(chosen, axis=0, keepdims=True)
    counts_ref[...] = carry_ref[...]


def _mixer_tail(attn, u, g, x2d, poolw_bd, poolb, pools, wba, wbp, wout,
                post1, gate1, pre2, sc2, sh2, rw_pad, rb_pad, *, seq, tm=512):
    T, D = x2d.shape
    d_pool = u.shape[1]
    tps = seq // tm
    hpt = tm // POOL_HALO

    def const(shape):
        return pl.BlockSpec(shape, lambda i: (0,) * len(shape))

    row = lambda w: pl.BlockSpec((tm, w), lambda i: (i, 0))
    vec = pl.BlockSpec((1, 1, D), lambda i: (i // tps, 0, 0))
    return pl.pallas_call(
        functools.partial(_mixer_tail_kernel, tm=tm, tps=tps),
        grid=(T // tm,),
        in_specs=[
            row(SB_WIDTH), row(d_pool),
            pl.BlockSpec((POOL_HALO, d_pool), lambda i: (jnp.maximum(i * hpt - 1, 0), 0)),
            row(2 * D), row(D),
            const((d_pool, d_pool)), const((1, d_pool)), const((1, d_pool)),
            const((SB_WIDTH, D)), const((d_pool, D)), const((D, D)),
            const((1, D)), vec, const((1, D)), vec, vec,
            const((D, LANES)), const((1, LANES)),
        ],
        out_specs=[row(D), row(D), row(LANES), const((1, LANES))],
        out_shape=[
            jax.ShapeDtypeStruct((T, D), F32),
            jax.ShapeDtypeStruct((T, D), F32),
            jax.ShapeDtypeStruct((T, LANES), F32),
            jax.ShapeDtypeStruct((1, LANES), F32),
        ],
        scratch_shapes=[pltpu.VMEM((1, LANES), F32)],
        compiler_params=pltpu.CompilerParams(
            dimension_semantics=("arbitrary",), vmem_limit_bytes=VMEM_LIMIT),
        name="mixer_tail",
    )(attn, u, u, g, x2d, poolw_bd, poolb, pools, wba, wbp, wout,
      post1, gate1, pre2, sc2, sh2, rw_pad, rb_pad)


def _row_map_kernel(dest_ref, pad_ref, inv_ref, *, chunk, n_pairs):
    i = pl.program_id(0)

    @pl.when(i == 0)
    def _():
        def pad(p, c):
            inv_ref[pad_ref[p]] = n_pairs + p
            return c

        lax.fori_loop(0, pad_ref.shape[0], pad, 0, unroll=8)

    def real(j, c):
        inv_ref[dest_ref[j]] = i * chunk + j
        return c

    lax.fori_loop(0, chunk, real, 0, unroll=8)


def _row_map(dest, pad_rows, *, chunk=1024):
    n_pairs = dest.shape[0]
    n_rows = n_pairs + pad_rows.shape[0]
    return pl.pallas_call(
        functools.partial(_row_map_kernel, chunk=chunk, n_pairs=n_pairs),
        grid=(n_pairs // chunk,),
        in_specs=[
            pl.BlockSpec((chunk,), lambda i: (i,), memory_space=pltpu.SMEM),
            pl.BlockSpec(memory_space=pltpu.SMEM),
        ],
        out_specs=pl.BlockSpec(memory_space=pltpu.SMEM),
        out_shape=jax.ShapeDtypeStruct((n_rows,), jnp.int32),
        compiler_params=pltpu.CompilerParams(dimension_semantics=("arbitrary",)),
        name="moe_row_map",
    )(dest, pad_rows)


def _ffn_kernel(be_ref, slots_a_ref, slots_b_ref, slots_prev_ref, slots_cur_ref,
                toks_0_ref, toks_1_ref, toks_ahead_ref, h_ref,
                wgu_ref, bgu_ref, wdn_ref, bdn_ref, yg_ref,
                wgu_bf, wdn_bf, xbuf0, xbuf1, xbuf2, stage0, stage1, stage2, sem_g, sem_s,
                *, n_blocks):
    b = pl.program_id(0)
    xbuf = (xbuf0, xbuf1, xbuf2)
    stage = (stage0, stage1, stage2)
    depth = len(xbuf)
    rb = xbuf0.shape[0]
    de = wdn_ref.shape[2]
    last = n_blocks - 1

    def gather(toks_ref, s, r):
        return pltpu.make_async_copy(
            h_ref.at[pl.ds(toks_ref[0, 0, r], 1), :], xbuf[s].at[pl.ds(r, 1), :], sem_g.at[s])

    def scatter(slots_ref, s, r):
        return pltpu.make_async_copy(
            stage[s].at[pl.ds(r, 1), :], yg_ref.at[pl.ds(slots_ref[0, 0, r], 1), :], sem_s.at[s])

    def gathered(s):
        return pltpu.make_async_copy(h_ref.at[pl.ds(0, rb), :], xbuf[s], sem_g.at[s])

    def scattered(s):
        return pltpu.make_async_copy(stage[s], yg_ref.at[pl.ds(0, rb), :], sem_s.at[s])

    def start_all(make, ref, s):
        def one(r, c):
            make(ref, s, r).start()
            return c

        lax.fori_loop(0, rb, one, 0, unroll=8)

    @pl.when(b == 0)
    def _():
        for st in stage:
            st[...] = jnp.zeros_like(st)
        start_all(gather, toks_0_ref, 0)
        start_all(gather, toks_1_ref, 1)
        start_all(scatter, slots_a_ref, 0)
        start_all(scatter, slots_b_ref, 1)

    @pl.when((b == 0) | (be_ref[b] != be_ref[jnp.maximum(b - 1, 0)]))
    def _():
        wgu_bf[...] = wgu_ref[0, 0].astype(BF16)
        wdn_bf[...] = wdn_ref[0, 0].astype(BF16)

    def block_step(s):
        gathered(s).wait()
        for r in range(rb):
            gather(toks_ahead_ref, (s + 2) % depth, r).start()
            scatter(slots_prev_ref, (s - 1) % depth, r).start()
        x = xbuf[s][...].astype(BF16)
        gu = jnp.dot(x, wgu_bf[...], preferred_element_type=F32) + bgu_ref[0, 0]
        gate = jnp.minimum(gu[:, 0:de], SWIGLU_LIMIT)
        up = jnp.clip(gu[:, de:2 * de], -SWIGLU_LIMIT, SWIGLU_LIMIT)
        act = (up + 1.0) * gate * jax.nn.sigmoid(SWIGLU_ALPHA * gate)
        y = jnp.dot(act.astype(BF16), wdn_bf[...], preferred_element_type=F32) + bdn_ref[0, 0]
        scattered(s).wait()
        stage[s][...] = y

    for s in range(depth):
        pl.when(b % depth == s)(functools.partial(block_step, s))

    @pl.when(b == last)
    def _():
        s = last % depth
        start_all(scatter, slots_cur_ref, s)
        for k in range(depth):
            scattered(k).wait()
        gathered((s + 1) % depth).wait()
        gathered((s + 2) % depth).wait()


N_SPARE_BLOCKS = 3


def _ffn(block_e, toks, slots, h2, w_gu, b_gu, w_dn, b_dn, layer):
    T, D = h2.shape
    L, E, _, n_gu = w_gu.shape
    de = w_dn.shape[2]
    n_blocks = toks.shape[0] - N_SPARE_BLOCKS
    n_out = (n_blocks + N_SPARE_BLOCKS) * ROW_BLOCK
    first = N_SPARE_BLOCKS
    assert n_blocks >= 3

    def view(index):
        return pl.BlockSpec((1, 1, ROW_BLOCK), lambda b, be: (index(b), 0, 0),
                            memory_space=pltpu.SMEM)

    grid_spec = pltpu.PrefetchScalarGridSpec(
        num_scalar_prefetch=1,
        grid=(n_blocks,),
        in_specs=[
            view(lambda b: 0), view(lambda b: 1),
            view(lambda b: b + first - 1),
            view(lambda b: b + first),
            view(lambda b: first), view(lambda b: first + 1),
            view(lambda b: jnp.minimum(b + 2, n_blocks - 1) + first),
            pl.BlockSpec(memory_space=pl.ANY),
            pl.BlockSpec((1, 1, D, n_gu), lambda b, be: (layer, be[b], 0, 0)),
            pl.BlockSpec((1, 1, 1, n_gu), lambda b, be: (layer, be[b], 0, 0)),
            pl.BlockSpec((1, 1, de, D), lambda b, be: (layer, be[b], 0, 0)),
            pl.BlockSpec((1, 1, 1, D), lambda b, be: (layer, be[b], 0, 0)),
        ],
        out_specs=pl.BlockSpec(memory_space=pl.ANY),
        scratch_shapes=[pltpu.VMEM((D, n_gu), BF16), pltpu.VMEM((de, D), BF16)]
        + [pltpu.VMEM((ROW_BLOCK, D), F32)] * 6
        + [pltpu.SemaphoreType.DMA((3,)), pltpu.SemaphoreType.DMA((3,))],
    )
    return pl.pallas_call(
        functools.partial(_ffn_kernel, n_blocks=n_blocks),
        grid_spec=grid_spec,
        out_shape=jax.ShapeDtypeStruct((n_out, D), F32),
        compiler_params=pltpu.CompilerParams(
            dimension_semantics=("arbitrary",), vmem_limit_bytes=VMEM_LIMIT),
        name="moe_ffn",
    )(block_e, slots, slots, slots, slots, toks, toks, toks, h2,
      w_gu, b_gu.reshape(L, E, 1, n_gu), w_dn, b_dn.reshape(L, E, 1, D))


def _combine_kernel(*refs):
    yk_refs = refs[:TOP_K]
    meta_ref, x1_ref, post2_ref, gate2_ref, o_ref = refs[TOP_K:]
    y = meta_ref[:, META_G:META_G + 1] * yk_refs[0][...]
    for kk in range(1, TOP_K):
        y = y + meta_ref[:, META_G + kk:META_G + kk + 1] * yk_refs[kk][...]
    o_ref[...] = x1_ref[...] + gate2_ref[0] * (y * _rms(y) * post2_ref[...])


def _combine(yg, meta, x1, post2, gate2, *, seq, tm=256):
    T, D = x1.shape
    tps = seq // tm
    steps = T // tm
    return pl.pallas_call(
        _combine_kernel,
        grid=(steps,),
        in_specs=[pl.BlockSpec((tm, D), functools.partial(lambda i, k: (k * steps + i, 0), k=k))
                  for k in range(TOP_K)] + [
            pl.BlockSpec((tm, LANES), lambda i: (i, 0)),
            pl.BlockSpec((tm, D), lambda i: (i, 0)),
            pl.BlockSpec((1, D), lambda i: (0, 0)),
            pl.BlockSpec((1, 1, D), lambda i: (i // tps, 0, 0)),
        ],
        out_specs=pl.BlockSpec((tm, D), lambda i: (i, 0)),
        out_shape=jax.ShapeDtypeStruct((T, D), F32),
        compiler_params=pltpu.CompilerParams(
            dimension_semantics=("arbitrary",), vmem_limit_bytes=VMEM_LIMIT),
        name="moe_combine",
    )(*([yg] * TOP_K), meta, x1, post2, gate2)


def _block_diag(w):
    g, a, b = w.shape
    out = jnp.zeros((g * a, g * b), w.dtype)
    for i in range(g):
        out = out.at[i * a:(i + 1) * a, i * b:(i + 1) * b].set(w[i])
    return out


def kernel(x, c, ada_w, ada_b, pre1_g, post1_g, pre2_g, post2_g, w_in, pool_w, pool_b, pool_scale,
           w_br_attn, w_br_pool, w_out, router_w, router_b, w_gu, b_gu, w_dn, b_dn):
    B, S, D = x.shape
    L = ada_w.shape[0]
    T = B * S
    d_pool = D - SB_WIDTH
    n_rows = T * TOP_K + N_EXPERTS * ROW_BLOCK
    n_blocks = n_rows // ROW_BLOCK

    c_pad = jnp.pad(c, ((0, 8 - B), (0, 0)))
    mod = _ada(c_pad, ada_w, ada_b)[:, :B]

    x2d = x.reshape(T, D)
    for l in range(L):
        shift1, scale1, gate1, shift2, scale2, gate2 = [
            mod[l, :, i * D:(i + 1) * D].reshape(B, 1, D) for i in range(6)]

        qkv, u, g = _in_proj(x2d, pre1_g[l], scale1, shift1, w_in[l].astype(BF16), seq=S)
        attn = _attention(qkv.reshape(B, S, 3 * SB_WIDTH)).reshape(T, SB_WIDTH)

        rw_pad = jnp.pad(router_w[l], ((0, 0), (0, LANES - N_EXPERTS)))
        rb_pad = jnp.pad(router_b[l], (0, LANES - N_EXPERTS), constant_values=-jnp.inf)
        x1, h2, meta, counts = _mixer_tail(
            attn, u, g, x2d,
            _block_diag(pool_w[l]).astype(BF16), pool_b[l].reshape(1, d_pool),
            pool_scale[l].reshape(1, d_pool),
            w_br_attn[l].astype(BF16), w_br_pool[l].astype(BF16), w_out[l].astype(BF16),
            post1_g[l].reshape(1, D), gate1, pre2_g[l].reshape(1, D), scale2, shift2,
            rw_pad, rb_pad.reshape(1, LANES), seq=S)

        e_idx = meta[:, META_E:META_E + TOP_K].astype(jnp.int32)
        rank = meta[:, META_R:META_R + TOP_K].astype(jnp.int32)
        cnt = counts[0, :N_EXPERTS].astype(jnp.int32)
        padded = ((cnt + ROW_BLOCK - 1) // ROW_BLOCK) * ROW_BLOCK
        pend = jnp.cumsum(padded)
        pstart = pend - padded
        dest = (pstart[e_idx] + rank).reshape(-1)
        blk0 = jnp.arange(n_blocks, dtype=jnp.int32) * ROW_BLOCK
        block_e = jnp.minimum(jnp.sum(pend[None, :] <= blk0[:, None], axis=1), N_EXPERTS - 1).astype(jnp.int32)
        gap = padded - cnt
        gap_end = jnp.cumsum(gap)
        p = jnp.arange(n_rows - T * TOP_K, dtype=jnp.int32)
        pe = jnp.sum(gap_end[None, :] <= p[:, None], axis=1)
        pe_c = jnp.minimum(pe, N_EXPERTS - 1)
        in_expert = pstart[pe_c] + cnt[pe_c] + (p - (gap_end - gap)[pe_c])
        pad_rows = jnp.where(pe < N_EXPERTS, in_expert, pend[-1] + (p - gap_end[-1])).astype(jnp.int32)

        pair = _row_map(dest, pad_rows)
        real = pair < T * TOP_K
        tok = jnp.where(real, pair // TOP_K, 0)
        slot = jnp.where(real, (pair % TOP_K) * T + pair // TOP_K, pair)
        n_spare = N_SPARE_BLOCKS * ROW_BLOCK
        as_blocks = lambda v: v.reshape(n_blocks + N_SPARE_BLOCKS, 1, ROW_BLOCK)
        toks = as_blocks(jnp.concatenate([jnp.zeros((n_spare,), jnp.int32), tok]))
        slots = as_blocks(jnp.concatenate([n_rows + jnp.arange(n_spare, dtype=jnp.int32), slot]))
        yg = _ffn(block_e, toks, slots, h2, w_gu, b_gu, w_dn, b_dn, l)
        x2d = _combine(yg, meta, x1, post2_g[l].reshape(1, D), gate2, seq=S)
    return x2d.reshape(B, S, D)
```

```python
import functools

import jax
import jax.numpy as jnp
from jax import lax
from jax.experimental import pallas as pl
from jax.experimental.pallas import tpu as pltpu

F32 = jnp.float32
BF16 = jnp.bfloat16

SB_HEADS = 4
SB_HEAD_DIM = 64
SB_WIDTH = SB_HEADS * SB_HEAD_DIM
POOL_WINDOWS = (2, 4, 8, 16)
POOL_GROUPS = len(POOL_WINDOWS)
POOL_HALO = 16
N_EXPERTS = 32
TOP_K = 4
ROW_BLOCK = 256
SWIGLU_LIMIT = 7.0
SWIGLU_ALPHA = 1.702
NORM_EPS = 1e-6
LOG2_E = 1.4426950408889634
F32_UNDERFLOW_LOG2 = -150.0

LANES = 128
VMEM_LIMIT = 52 * 1024 * 1024

META_E, META_G, META_R = 0, 4, 8


def _rms(x):
    return lax.rsqrt(jnp.mean(x * x, axis=-1, keepdims=True) + NORM_EPS)


def _ada_kernel(c_ref, w_ref, b_ref, o_ref):
    c = c_ref[...]
    ca = c * jax.nn.sigmoid(c)
    o_ref[0] = jnp.dot(ca, w_ref[0], precision=lax.Precision.HIGHEST,
                       preferred_element_type=F32) + b_ref[0]


def _ada(c_pad, ada_w, ada_b):
    L, D, N = ada_w.shape
    bp = c_pad.shape[0]
    tn = 1536
    return pl.pallas_call(
        _ada_kernel,
        grid=(L, N // tn),
        in_specs=[
            pl.BlockSpec((bp, D), lambda l, j: (0, 0)),
            pl.BlockSpec((1, D, tn), lambda l, j: (l, 0, j)),
            pl.BlockSpec((1, 1, tn), lambda l, j: (l, 0, j)),
        ],
        out_specs=pl.BlockSpec((1, bp, tn), lambda l, j: (l, 0, j)),
        out_shape=jax.ShapeDtypeStruct((L, bp, N), F32),
        compiler_params=pltpu.CompilerParams(
            dimension_semantics=("arbitrary", "arbitrary"), vmem_limit_bytes=VMEM_LIMIT),
        name="ada_mod",
    )(c_pad, ada_w, ada_b.reshape(L, 1, N))


def _in_proj_kernel(x_ref, g_ref, sc_ref, sh_ref, w_ref, qkv_ref, u_ref, gate_ref, *, d_pool):
    x = x_ref[...]
    h = (x * _rms(x) * g_ref[...]) * (1.0 + sc_ref[0]) + sh_ref[0]
    hb = h.astype(BF16)
    n_qkv = 3 * SB_WIDTH
    qkv = jnp.dot(hb, w_ref[:, 0:n_qkv], preferred_element_type=F32)
    col = lax.broadcasted_iota(jnp.int32, (1, n_qkv), 1)
    qkv = jnp.where(col < SB_WIDTH, qkv * (LOG2_E * SB_HEAD_DIM ** -0.5), qkv)
    qkv_ref[...] = qkv.astype(BF16)
    u_ref[...] = jnp.dot(hb, w_ref[:, n_qkv:n_qkv + d_pool], preferred_element_type=F32)
    gl = jnp.dot(hb, w_ref[:, n_qkv + d_pool:], preferred_element_type=F32)
    gate_ref[...] = jax.nn.sigmoid(gl).astype(BF16)


def _in_proj(x2d, g, scale, shift, w_bf, *, seq, tm=512):
    T, D = x2d.shape
    n_in = w_bf.shape[1]
    n_qkv = 3 * SB_WIDTH
    d_pool = D - SB_WIDTH
    n_gate = n_in - n_qkv - d_pool
    tps = seq // tm
    vec = pl.BlockSpec((1, 1, D), lambda i: (i // tps, 0, 0))
    return pl.pallas_call(
        functools.partial(_in_proj_kernel, d_pool=d_pool),
        grid=(T // tm,),
        in_specs=[
            pl.BlockSpec((tm, D), lambda i: (i, 0)),
            pl.BlockSpec((1, D), lambda i: (0, 0)),
            vec, vec,
            pl.BlockSpec((D, n_in), lambda i: (0, 0)),
        ],
        out_specs=[
            pl.BlockSpec((tm, n_qkv), lambda i: (i, 0)),
            pl.BlockSpec((tm, d_pool), lambda i: (i, 0)),
            pl.BlockSpec((tm, n_gate), lambda i: (i, 0)),
        ],
        out_shape=[
            jax.ShapeDtypeStruct((T, n_qkv), BF16),
            jax.ShapeDtypeStruct((T, d_pool), F32),
            jax.ShapeDtypeStruct((T, n_gate), BF16),
        ],
        compiler_params=pltpu.CompilerParams(
            dimension_semantics=("arbitrary",), vmem_limit_bytes=VMEM_LIMIT),
        name="prenorm_in_proj",
    )(x2d, g.reshape(1, D), scale, shift, w_bf)


def _attn_kernel(q_ref, k_ref, v_ref, o_ref, acc_ref, carry_ref, *, blk):
    i = pl.program_id(1)
    width = q_ref.shape[2]
    heads = width // SB_HEAD_DIM
    head_of_lane = lax.broadcasted_iota(jnp.int32, (1, width), 1) // SB_HEAD_DIM
    q = q_ref[0]
    zero = jnp.zeros((), BF16)
    qs = jnp.concatenate([jnp.where(head_of_lane == h, q, zero) for h in range(heads)], axis=0)
    row = lax.broadcasted_iota(jnp.int32, (blk, blk), 0)
    col = lax.broadcasted_iota(jnp.int32, (blk, blk), 1)
    tri = (row > col).astype(BF16)
    causal = jnp.concatenate([col < row] * heads, axis=0)

    def step(j, mask):
        ks = pl.multiple_of(j * blk, blk)
        k = k_ref[0, pl.ds(ks, blk), :]
        v = v_ref[0, pl.ds(ks, blk), :]
        t = lax.dot_general(qs, k, (((1,), (1,)), ((), ())), preferred_element_type=F32)
        soft = jnp.log2(1.0 + jnp.exp2(-jnp.abs(t)))
        log_take = jnp.minimum(t, 0.0) - soft
        log_keep = log_take - t
        if mask is not None:
            log_keep = jnp.where(mask, log_keep, 0.0)
        later = jnp.dot(log_keep.astype(BF16), tri, preferred_element_type=F32)
        a = jnp.exp2(log_take + later + carry_ref[...])
        if mask is not None:
            a = jnp.where(mask, a, 0.0)
        a = a.astype(BF16)
        a_wide = jnp.concatenate([a[h * blk:(h + 1) * blk] for h in range(heads)], axis=1)
        v_heads = jnp.concatenate([jnp.where(head_of_lane == h, v, zero) for h in range(heads)], axis=0)
        acc_ref[...] += jnp.dot(a_wide, v_heads, preferred_element_type=F32)
        carry_ref[...] += jnp.sum(log_keep, axis=-1, keepdims=True)

    acc_ref[...] = jnp.zeros_like(acc_ref)
    carry_ref[...] = jnp.zeros_like(carry_ref)
    step(i, causal)

    def more(state):
        jj, carry_max = state
        return jnp.logical_and(jj < i, carry_max > F32_UNDERFLOW_LOG2)

    def body(state):
        jj, _ = state
        step(i - 1 - jj, None)
        return jj + 1, jnp.max(carry_ref[...])

    lax.while_loop(more, body, (jnp.int32(0), jnp.max(carry_ref[...])))
    o_ref[0] = acc_ref[...].astype(o_ref.dtype)


def _attention(qkv, *, blk=256):
    B, S, _ = qkv.shape
    whole = lambda part: pl.BlockSpec((1, S, SB_WIDTH), lambda b, i: (b, 0, part),
                                      pipeline_mode=pl.Buffered(1))
    return pl.pallas_call(
        functools.partial(_attn_kernel, blk=blk),
        grid=(B, S // blk),
        in_specs=[
            pl.BlockSpec((1, blk, SB_WIDTH), lambda b, i: (b, i, 0)),
            whole(1), whole(2),
        ],
        out_specs=pl.BlockSpec((1, blk, SB_WIDTH), lambda b, i: (b, i, 0)),
        out_shape=jax.ShapeDtypeStruct((B, S, SB_WIDTH), BF16),
        scratch_shapes=[pltpu.VMEM((blk, SB_WIDTH), F32),
                        pltpu.VMEM((SB_HEADS * blk, 1), F32)],
        compiler_params=pltpu.CompilerParams(
            dimension_semantics=("arbitrary", "arbitrary"), vmem_limit_bytes=VMEM_LIMIT),
        name="sb_attention",
    )(qkv, qkv, qkv)


def _mixer_tail_kernel(attn_ref, u_ref, uh_ref, g_ref, x_ref,
                       poolw_ref, poolb_ref, pools_ref, wba_ref, wbp_ref, wout_ref,
                       post1_ref, gate1_ref, pre2_ref, sc2_ref, sh2_ref, rw_ref, rb_ref,
                       x1_ref, h2_ref, meta_ref, counts_ref, carry_ref, *, tm, tps):
    i = pl.program_id(0)
    D = x_ref.shape[1]
    d_pool = u_ref.shape[1]
    gdim = d_pool // POOL_GROUPS
    seq_tile = i % tps

    u = u_ref[...]
    halo = jnp.where(seq_tile == 0, 0.0, uh_ref[...])
    s = jnp.concatenate([halo, u], axis=0)
    sums = []
    span = 1
    for w in POOL_WINDOWS:
        while span < w:
            s = s + pltpu.roll(s, span, 0)
            span *= 2
        sums.append(s[POOL_HALO:])
    pos1 = (seq_tile * tm + 1 + lax.broadcasted_iota(jnp.int32, (tm, 1), 0)).astype(F32)
    colp = lax.broadcasted_iota(jnp.int32, (1, d_pool), 1)
    mean = sums[-1] / jnp.minimum(pos1, float(POOL_WINDOWS[-1]))
    for gi in range(POOL_GROUPS - 2, -1, -1):
        mean = jnp.where(colp < (gi + 1) * gdim,
                         sums[gi] / jnp.minimum(pos1, float(POOL_WINDOWS[gi])), mean)
    mixed = mean - u
    yp = jnp.dot(mixed.astype(BF16), poolw_ref[...], preferred_element_type=F32)
    pool = ((yp + poolb_ref[...]) * pools_ref[...]).astype(BF16)

    pa = jnp.dot(attn_ref[...], wba_ref[...], preferred_element_type=F32)
    pp = jnp.dot(pool, wbp_ref[...], preferred_element_type=F32)
    merged = g_ref[:, 0:D].astype(F32) * pa + g_ref[:, D:2 * D].astype(F32) * pp
    y = jnp.dot(merged.astype(BF16), wout_ref[...], preferred_element_type=F32)
    x1 = x_ref[...] + gate1_ref[0] * (y * _rms(y) * post1_ref[...])
    x1_ref[...] = x1
    h2 = (x1 * _rms(x1) * pre2_ref[...]) * (1.0 + sc2_ref[0]) + sh2_ref[0]
    h2_ref[...] = h2

    h_hi = h2.astype(BF16)
    h_lo = (h2 - h_hi.astype(F32)).astype(BF16)
    logits = (jnp.dot(h_hi, rw_ref[0], preferred_element_type=F32)
              + (jnp.dot(h_lo, rw_ref[0], preferred_element_type=F32)
                 + jnp.dot(h_hi, rw_ref[1], preferred_element_type=F32))) + rb_ref[...]
    lane = lax.broadcasted_iota(jnp.int32, (tm, LANES), 1).astype(F32)
    work = logits
    vals, idxs, hots = [], [], []
    for _ in range(TOP_K):
        m = jnp.max(work, axis=-1, keepdims=True)
        idx = jnp.min(jnp.where(work == m, lane, float(LANES)), axis=-1, keepdims=True)
        hot = lane == idx
        work = jnp.where(hot, -jnp.inf, work)
        vals.append(m)
        idxs.append(idx)
        hots.append(hot)
    exps = [jnp.exp(vk - vals[0]) for vk in vals]
    denom = exps[0] + exps[1] + exps[2] + exps[3]
    chosen = (hots[0] | hots[1] | hots[2] | hots[3]).astype(F32)

    @pl.when(i == 0)
    def _():
        carry_ref[...] = jnp.zeros_like(carry_ref)

    trow = lax.broadcasted_iota(jnp.int32, (tm, tm), 0)
    tcol = lax.broadcasted_iota(jnp.int32, (tm, tm), 1)
    before = (tcol < trow).astype(BF16)
    rank_all = jnp.dot(before, chosen.astype(BF16), preferred_element_type=F32) + carry_ref[...]
    meta = jnp.zeros((tm, LANES), F32)
    for kk in range(TOP_K):
        rk = jnp.sum(jnp.where(hots[kk], rank_all, 0.0), axis=-1, keepdims=True)
        meta = jnp.where(lane == float(META_E + kk), idxs[kk], meta)
        meta = jnp.where(lane == float(META_G + kk), exps[kk] / denom, meta)
        meta = jnp.where(lane == float(META_R + kk), rk, meta)
    meta_ref[...] = meta
    carry_ref[...] += jnp.sum(chosen, axis=0, keepdims=True)
    counts_ref[...] = carry_ref[...]


def _mixer_tail(attn, u, g, x2d, poolw_bd, poolb, pools, wba, wbp, wout,
                post1, gate1, pre2, sc2, sh2, rw_pad, rb_pad, *, seq, tm=512):
    T, D = x2d.shape
    d_pool = u.shape[1]
    tps = seq // tm
    hpt = tm // POOL_HALO

    def const(shape):
        return pl.BlockSpec(shape, lambda i: (0,) * len(shape))

    row = lambda w: pl.BlockSpec((tm, w), lambda i: (i, 0))
    vec = pl.BlockSpec((1, 1, D), lambda i: (i // tps, 0, 0))
    return pl.pallas_call(
        functools.partial(_mixer_tail_kernel, tm=tm, tps=tps),
        grid=(T // tm,),
        in_specs=[
            row(SB_WIDTH), row(d_pool),
            pl.BlockSpec((POOL_HALO, d_pool), lambda i: (jnp.maximum(i * hpt - 1, 0), 0)),
            row(2 * D), row(D),
            const((d_pool, d_pool)), const((1, d_pool)), const((1, d_pool)),
            const((SB_WIDTH, D)), const((d_pool, D)), const((D, D)),
            const((1, D)), vec, const((1, D)), vec, vec,
            const((2, D, LANES)), const((1, LANES)),
        ],
        out_specs=[row(D), row(D), row(LANES), const((1, LANES))],
        out_shape=[
            jax.ShapeDtypeStruct((T, D), F32),
            jax.ShapeDtypeStruct((T, D), F32),
            jax.ShapeDtypeStruct((T, LANES), F32),
            jax.ShapeDtypeStruct((1, LANES), F32),
        ],
        scratch_shapes=[pltpu.VMEM((1, LANES), F32)],
        compiler_params=pltpu.CompilerParams(
            dimension_semantics=("arbitrary",), vmem_limit_bytes=VMEM_LIMIT),
        name="mixer_tail",
    )(attn, u, u, g, x2d, poolw_bd, poolb, pools, wba, wbp, wout,
      post1, gate1, pre2, sc2, sh2, rw_pad, rb_pad)


def _row_map_kernel(dest_ref, pad_ref, inv_ref, *, chunk, n_pairs):
    i = pl.program_id(0)

    @pl.when(i == 0)
    def _():
        def pad(p, c):
            inv_ref[pad_ref[p]] = n_pairs + p
            return c

        lax.fori_loop(0, pad_ref.shape[0], pad, 0, unroll=8)

    def real(j, c):
        inv_ref[dest_ref[j]] = i * chunk + j
        return c

    lax.fori_loop(0, chunk, real, 0, unroll=8)


def _row_map(dest, pad_rows, *, chunk=1024):
    n_pairs = dest.shape[0]
    n_rows = n_pairs + pad_rows.shape[0]
    return pl.pallas_call(
        functools.partial(_row_map_kernel, chunk=chunk, n_pairs=n_pairs),
        grid=(n_pairs // chunk,),
        in_specs=[
            pl.BlockSpec((chunk,), lambda i: (i,), memory_space=pltpu.SMEM),
            pl.BlockSpec(memory_space=pltpu.SMEM),
        ],
        out_specs=pl.BlockSpec(memory_space=pltpu.SMEM),
        out_shape=jax.ShapeDtypeStruct((n_rows,), jnp.int32),
        compiler_params=pltpu.CompilerParams(dimension_semantics=("arbitrary",)),
        name="moe_row_map",
    )(dest, pad_rows)


def _ffn_kernel(be_ref, slots_a_ref, slots_b_ref, slots_prev_ref, slots_cur_ref,
                toks_0_ref, toks_1_ref, toks_ahead_ref, h_ref,
                wgu_ref, bgu_ref, wdn_ref, bdn_ref, yg_ref,
                wgu_bf, wdn_bf, xbuf0, xbuf1, xbuf2, stage0, stage1, stage2, sem_g, sem_s,
                *, n_blocks):
    b = pl.program_id(0)
    xbuf = (xbuf0, xbuf1, xbuf2)
    stage = (stage0, stage1, stage2)
    depth = len(xbuf)
    rb = xbuf0.shape[0]
    de = wdn_ref.shape[2]
    last = n_blocks - 1

    def gather(toks_ref, s, r):
        return pltpu.make_async_copy(
            h_ref.at[pl.ds(toks_ref[0, 0, r], 1), :], xbuf[s].at[pl.ds(r, 1), :], sem_g.at[s])

    def scatter(slots_ref, s, r):
        return pltpu.make_async_copy(
            stage[s].at[pl.ds(r, 1), :], yg_ref.at[pl.ds(slots_ref[0, 0, r], 1), :], sem_s.at[s])

    def gathered(s):
        return pltpu.make_async_copy(h_ref.at[pl.ds(0, rb), :], xbuf[s], sem_g.at[s])

    def scattered(s):
        return pltpu.make_async_copy(stage[s], yg_ref.at[pl.ds(0, rb), :], sem_s.at[s])

    def start_all(make, ref, s):
        def one(r, c):
            make(ref, s, r).start()
            return c

        lax.fori_loop(0, rb, one, 0, unroll=8)

    @pl.when(b == 0)
    def _():
        for st in stage:
            st[...] = jnp.zeros_like(st)
        start_all(gather, toks_0_ref, 0)
        start_all(gather, toks_1_ref, 1)
        start_all(scatter, slots_a_ref, 0)
        start_all(scatter, slots_b_ref, 1)

    @pl.when((b == 0) | (be_ref[b] != be_ref[jnp.maximum(b - 1, 0)]))
    def _():
        wgu_bf[...] = wgu_ref[0, 0].astype(BF16)
        wdn_bf[...] = wdn_ref[0, 0].astype(BF16)

    def block_step(s):
        gathered(s).wait()
        for r in range(rb):
            gather(toks_ahead_ref, (s + 2) % depth, r).start(priority=r % 2)
            scatter(slots_prev_ref, (s - 1) % depth, r).start(priority=r % 2)
        x = xbuf[s][...].astype(BF16)
        gu = jnp.dot(x, wgu_bf[...], preferred_element_type=F32) + bgu_ref[0, 0]
        gate = jnp.minimum(gu[:, 0:de], SWIGLU_LIMIT)
        up = jnp.clip(gu[:, de:2 * de], -SWIGLU_LIMIT, SWIGLU_LIMIT)
        act = (up + 1.0) * gate * jax.nn.sigmoid(SWIGLU_ALPHA * gate)
        y = jnp.dot(act.astype(BF16), wdn_bf[...], preferred_element_type=F32) + bdn_ref[0, 0]
        scattered(s).wait()
        stage[s][...] = y

    for s in range(depth):
        pl.when(b % depth == s)(functools.partial(block_step, s))

    @pl.when(b == last)
    def _():
        s = last % depth
        start_all(scatter, slots_cur_ref, s)
        for k in range(depth):
            scattered(k).wait()
        gathered((s + 1) % depth).wait()
        gathered((s + 2) % depth).wait()


N_SPARE_BLOCKS = 3


def _ffn(block_e, toks, slots, h2, w_gu, b_gu, w_dn, b_dn, layer):
    T, D = h2.shape
    L, E, _, n_gu = w_gu.shape
    de = w_dn.shape[2]
    n_blocks = toks.shape[0] - N_SPARE_BLOCKS
    n_out = (n_blocks + N_SPARE_BLOCKS) * ROW_BLOCK
    first = N_SPARE_BLOCKS
    assert n_blocks >= 3

    def view(index):
        return pl.BlockSpec((1, 1, ROW_BLOCK), lambda b, be: (index(b), 0, 0),
                            memory_space=pltpu.SMEM)

    grid_spec = pltpu.PrefetchScalarGridSpec(
        num_scalar_prefetch=1,
        grid=(n_blocks,),
        in_specs=[
            view(lambda b: 0), view(lambda b: 1),
            view(lambda b: b + first - 1),
            view(lambda b: b + first),
            view(lambda b: first), view(lambda b: first + 1),
            view(lambda b: jnp.minimum(b + 2, n_blocks - 1) + first),
            pl.BlockSpec(memory_space=pl.ANY),
            pl.BlockSpec((1, 1, D, n_gu), lambda b, be: (layer, be[b], 0, 0)),
            pl.BlockSpec((1, 1, 1, n_gu), lambda b, be: (layer, be[b], 0, 0)),
            pl.BlockSpec((1, 1, de, D), lambda b, be: (layer, be[b], 0, 0)),
            pl.BlockSpec((1, 1, 1, D), lambda b, be: (layer, be[b], 0, 0)),
        ],
        out_specs=pl.BlockSpec(memory_space=pl.ANY),
        scratch_shapes=[pltpu.VMEM((D, n_gu), BF16), pltpu.VMEM((de, D), BF16)]
        + [pltpu.VMEM((ROW_BLOCK, D), F32)] * 6
        + [pltpu.SemaphoreType.DMA((3,)), pltpu.SemaphoreType.DMA((3,))],
    )
    return pl.pallas_call(
        functools.partial(_ffn_kernel, n_blocks=n_blocks),
        grid_spec=grid_spec,
        out_shape=jax.ShapeDtypeStruct((n_out, D), F32),
        compiler_params=pltpu.CompilerParams(
            dimension_semantics=("arbitrary",), vmem_limit_bytes=VMEM_LIMIT),
        name="moe_ffn",
    )(block_e, slots, slots, slots, slots, toks, toks, toks, h2,
      w_gu, b_gu.reshape(L, E, 1, n_gu), w_dn, b_dn.reshape(L, E, 1, D))


def _combine_kernel(*refs):
    yk_refs = refs[:TOP_K]
    meta_ref, x1_ref, post2_ref, gate2_ref, o_ref = refs[TOP_K:]
    y = meta_ref[:, META_G:META_G + 1] * yk_refs[0][...]
    for kk in range(1, TOP_K):
        y = y + meta_ref[:, META_G + kk:META_G + kk + 1] * yk_refs[kk][...]
    o_ref[...] = x1_ref[...] + gate2_ref[0] * (y * _rms(y) * post2_ref[...])


def _combine(yg, meta, x1, post2, gate2, *, seq, tm=256):
    T, D = x1.shape
    tps = seq // tm
    steps = T // tm
    return pl.pallas_call(
        _combine_kernel,
        grid=(steps,),
        in_specs=[pl.BlockSpec((tm, D), functools.partial(lambda i, k: (k * steps + i, 0), k=k))
                  for k in range(TOP_K)] + [
            pl.BlockSpec((tm, LANES), lambda i: (i, 0)),
            pl.BlockSpec((tm, D), lambda i: (i, 0)),
            pl.BlockSpec((1, D), lambda i: (0, 0)),
            pl.BlockSpec((1, 1, D), lambda i: (i // tps, 0, 0)),
        ],
        out_specs=pl.BlockSpec((tm, D), lambda i: (i, 0)),
        out_shape=jax.ShapeDtypeStruct((T, D), F32),
        compiler_params=pltpu.CompilerParams(
            dimension_semantics=("arbitrary",), vmem_limit_bytes=VMEM_LIMIT),
        name="moe_combine",
    )(*([yg] * TOP_K), meta, x1, post2, gate2)


def _block_diag(w):
    g, a, b = w.shape
    out = jnp.zeros((g * a, g * b), w.dtype)
    for i in range(g):
        out = out.at[i * a:(i + 1) * a, i * b:(i + 1) * b].set(w[i])
    return out


def kernel(x, c, ada_w, ada_b, pre1_g, post1_g, pre2_g, post2_g, w_in, pool_w, pool_b, pool_scale,
           w_br_attn, w_br_pool, w_out, router_w, router_b, w_gu, b_gu, w_dn, b_dn):
    B, S, D = x.shape
    L = ada_w.shape[0]
    T = B * S
    d_pool = D - SB_WIDTH
    n_rows = T * TOP_K + N_EXPERTS * ROW_BLOCK
    n_blocks = n_rows // ROW_BLOCK

    c_pad = jnp.pad(c, ((0, 8 - B), (0, 0)))
    mod = _ada(c_pad, ada_w, ada_b)[:, :B]

    x2d = x.reshape(T, D)
    for l in range(L):
        shift1, scale1, gate1, shift2, scale2, gate2 = [
            mod[l, :, i * D:(i + 1) * D].reshape(B, 1, D) for i in range(6)]

        qkv, u, g = _in_proj(x2d, pre1_g[l], scale1, shift1, w_in[l].astype(BF16), seq=S)
        attn = _attention(qkv.reshape(B, S, 3 * SB_WIDTH)).reshape(T, SB_WIDTH)

        rw_pad = jnp.pad(router_w[l], ((0, 0), (0, LANES - N_EXPERTS)))
        rw_hi = rw_pad.astype(BF16)
        rw_pad = jnp.stack([rw_hi, (rw_pad - rw_hi.astype(F32)).astype(BF16)])
        rb_pad = jnp.pad(router_b[l], (0, LANES - N_EXPERTS), constant_values=-jnp.inf)
        x1, h2, meta, counts = _mixer_tail(
            attn, u, g, x2d,
            _block_diag(pool_w[l]).astype(BF16), pool_b[l].reshape(1, d_pool),
            pool_scale[l].reshape(1, d_pool),
            w_br_attn[l].astype(BF16), w_br_pool[l].astype(BF16), w_out[l].astype(BF16),
            post1_g[l].reshape(1, D), gate1, pre2_g[l].reshape(1, D), scale2, shift2,
            rw_pad, rb_pad.reshape(1, LANES), seq=S)

        e_idx = meta[:, META_E:META_E + TOP_K].astype(jnp.int32)
        rank = meta[:, META_R:META_R + TOP_K].astype(jnp.int32)
        cnt = counts[0, :N_EXPERTS].astype(jnp.int32)
        padded = ((cnt + ROW_BLOCK - 1) // ROW_BLOCK) * ROW_BLOCK
        pend = jnp.cumsum(padded)
        pstart = pend - padded
        dest = (pstart[e_idx] + rank).reshape(-1)
        blk0 = jnp.arange(n_blocks, dtype=jnp.int32) * ROW_BLOCK
        block_e = jnp.minimum(jnp.sum(pend[None, :] <= blk0[:, None], axis=1), N_EXPERTS - 1).astype(jnp.int32)
        gap = padded - cnt
        gap_end = jnp.cumsum(gap)
        p = jnp.arange(n_rows - T * TOP_K, dtype=jnp.int32)
        pe = jnp.sum(gap_end[None, :] <= p[:, None], axis=1)
        pe_c = jnp.minimum(pe, N_EXPERTS - 1)
        in_expert = pstart[pe_c] + cnt[pe_c] + (p - (gap_end - gap)[pe_c])
        pad_rows = jnp.where(pe < N_EXPERTS, in_expert, pend[-1] + (p - gap_end[-1])).astype(jnp.int32)

        pair = _row_map(dest, pad_rows)
        real = pair < T * TOP_K
        tok = jnp.where(real, pair // TOP_K, 0)
        slot = jnp.where(real, (pair % TOP_K) * T + pair // TOP_K, pair)
        n_spare = N_SPARE_BLOCKS * ROW_BLOCK
        as_blocks = lambda v: v.reshape(n_blocks + N_SPARE_BLOCKS, 1, ROW_BLOCK)
        toks = as_blocks(jnp.concatenate([jnp.zeros((n_spare,), jnp.int32), tok]))
        slots = as_blocks(jnp.concatenate([n_rows + jnp.arange(n_spare, dtype=jnp.int32), slot]))
        yg = _ffn(block_e, toks, slots, h2, w_gu, b_gu, w_dn, b_dn, l)
        x2d = _combine(yg, meta, x1, post2_g[l].reshape(1, D), gate2, seq=S)
    return x2d.reshape(B, S, D)
```

```python
import functools

import jax
import jax.numpy as jnp
from jax import lax
from jax.experimental import pallas as pl
from jax.experimental.pallas import tpu as pltpu

F32 = jnp.float32
BF16 = jnp.bfloat16

SB_HEADS = 4
SB_HEAD_DIM = 64
SB_WIDTH = SB_HEADS * SB_HEAD_DIM
POOL_WINDOWS = (2, 4, 8, 16)
POOL_GROUPS = len(POOL_WINDOWS)
POOL_HALO = 16
N_EXPERTS = 32
TOP_K = 4
ROW_BLOCK = 256
SWIGLU_LIMIT = 7.0
SWIGLU_ALPHA = 1.702
NORM_EPS = 1e-6
LOG2_E = 1.4426950408889634
F32_UNDERFLOW_LOG2 = -150.0

LANES = 128
VMEM_LIMIT = 52 * 1024 * 1024

META_E, META_G, META_R = 0, 4, 8
ROUTE_ROWS = 16


def _rms(x):
    return lax.rsqrt(jnp.mean(x * x, axis=-1, keepdims=True) + NORM_EPS)


def _ada_kernel(c_ref, w_ref, b_ref, o_ref):
    c = c_ref[...]
    ca = c * jax.nn.sigmoid(c)
    o_ref[0] = jnp.dot(ca, w_ref[0], precision=lax.Precision.HIGHEST,
                       preferred_element_type=F32) + b_ref[0]


def _ada(c_pad, ada_w, ada_b):
    L, D, N = ada_w.shape
    bp = c_pad.shape[0]
    tn = 1536
    return pl.pallas_call(
        _ada_kernel,
        grid=(L, N // tn),
        in_specs=[
            pl.BlockSpec((bp, D), lambda l, j: (0, 0)),
            pl.BlockSpec((1, D, tn), lambda l, j: (l, 0, j)),
            pl.BlockSpec((1, 1, tn), lambda l, j: (l, 0, j)),
        ],
        out_specs=pl.BlockSpec((1, bp, tn), lambda l, j: (l, 0, j)),
        out_shape=jax.ShapeDtypeStruct((L, bp, N), F32),
        compiler_params=pltpu.CompilerParams(
            dimension_semantics=("arbitrary", "arbitrary"), vmem_limit_bytes=VMEM_LIMIT),
        name="ada_mod",
    )(c_pad, ada_w, ada_b.reshape(L, 1, N))


def _in_proj_kernel(x_ref, g_ref, sc_ref, sh_ref, w_ref, qkv_ref, u_ref, gate_ref, *, d_pool):
    x = x_ref[...]
    h = (x * _rms(x) * g_ref[...]) * (1.0 + sc_ref[0]) + sh_ref[0]
    hb = h.astype(BF16)
    n_qkv = 3 * SB_WIDTH
    qkv = jnp.dot(hb, w_ref[:, 0:n_qkv], preferred_element_type=F32)
    col = lax.broadcasted_iota(jnp.int32, (1, n_qkv), 1)
    qkv = jnp.where(col < SB_WIDTH, qkv * (LOG2_E * SB_HEAD_DIM ** -0.5), qkv)
    qkv_ref[...] = qkv.astype(BF16)
    u_ref[...] = jnp.dot(hb, w_ref[:, n_qkv:n_qkv + d_pool], preferred_element_type=F32)
    gl = jnp.dot(hb, w_ref[:, n_qkv + d_pool:], preferred_element_type=F32)
    gate_ref[...] = jax.nn.sigmoid(gl).astype(BF16)


def _in_proj(x2d, g, scale, shift, w_bf, *, seq, tm=512):
    T, D = x2d.shape
    n_in = w_bf.shape[1]
    n_qkv = 3 * SB_WIDTH
    d_pool = D - SB_WIDTH
    n_gate = n_in - n_qkv - d_pool
    tps = seq // tm
    vec = pl.BlockSpec((1, 1, D), lambda i: (i // tps, 0, 0))
    return pl.pallas_call(
        functools.partial(_in_proj_kernel, d_pool=d_pool),
        grid=(T // tm,),
        in_specs=[
            pl.BlockSpec((tm, D), lambda i: (i, 0)),
            pl.BlockSpec((1, D), lambda i: (0, 0)),
            vec, vec,
            pl.BlockSpec((D, n_in), lambda i: (0, 0)),
        ],
        out_specs=[
            pl.BlockSpec((tm, n_qkv), lambda i: (i, 0)),
            pl.BlockSpec((tm, d_pool), lambda i: (i, 0)),
            pl.BlockSpec((tm, n_gate), lambda i: (i, 0)),
        ],
        out_shape=[
            jax.ShapeDtypeStruct((T, n_qkv), BF16),
            jax.ShapeDtypeStruct((T, d_pool), F32),
            jax.ShapeDtypeStruct((T, n_gate), BF16),
        ],
        compiler_params=pltpu.CompilerParams(
            dimension_semantics=("arbitrary",), vmem_limit_bytes=VMEM_LIMIT),
        name="prenorm_in_proj",
    )(x2d, g.reshape(1, D), scale, shift, w_bf)


def _attn_kernel(q_ref, k_ref, v_ref, o_ref, acc_ref, carry_ref, *, blk):
    i = pl.program_id(1)
    width = q_ref.shape[2]
    heads = width // SB_HEAD_DIM
    head_of_lane = lax.broadcasted_iota(jnp.int32, (1, width), 1) // SB_HEAD_DIM
    q = q_ref[0]
    zero = jnp.zeros((), BF16)
    qs = jnp.concatenate([jnp.where(head_of_lane == h, q, zero) for h in range(heads)], axis=0)
    row = lax.broadcasted_iota(jnp.int32, (blk, blk), 0)
    col = lax.broadcasted_iota(jnp.int32, (blk, blk), 1)
    tri = (row > col).astype(BF16)
    causal = jnp.concatenate([col < row] * heads, axis=0)

    def step(j, mask):
        ks = pl.multiple_of(j * blk, blk)
        k = k_ref[0, pl.ds(ks, blk), :]
        v = v_ref[0, pl.ds(ks, blk), :]
        t = lax.dot_general(qs, k, (((1,), (1,)), ((), ())), preferred_element_type=F32)
        soft = jnp.log2(1.0 + jnp.exp2(-jnp.abs(t)))
        log_take = jnp.minimum(t, 0.0) - soft
        log_keep = log_take - t
        if mask is not None:
            log_keep = jnp.where(mask, log_keep, 0.0)
        later = jnp.dot(log_keep.astype(BF16), tri, preferred_element_type=F32)
        a = jnp.exp2(log_take + later + carry_ref[...])
        if mask is not None:
            a = jnp.where(mask, a, 0.0)
        a = a.astype(BF16)
        a_wide = jnp.concatenate([a[h * blk:(h + 1) * blk] for h in range(heads)], axis=1)
        v_heads = jnp.concatenate([jnp.where(head_of_lane == h, v, zero) for h in range(heads)], axis=0)
        acc_ref[...] += jnp.dot(a_wide, v_heads, preferred_element_type=F32)
        carry_ref[...] += jnp.sum(log_keep, axis=-1, keepdims=True)

    acc_ref[...] = jnp.zeros_like(acc_ref)
    carry_ref[...] = jnp.zeros_like(carry_ref)
    step(i, causal)

    def more(state):
        jj, carry_max = state
        return jnp.logical_and(jj < i, carry_max > F32_UNDERFLOW_LOG2)

    def body(state):
        jj, _ = state
        step(i - 1 - jj, None)
        return jj + 1, jnp.max(carry_ref[...])

    lax.while_loop(more, body, (jnp.int32(0), jnp.max(carry_ref[...])))
    o_ref[0] = acc_ref[...].astype(o_ref.dtype)


def _attention(qkv, *, blk=256):
    B, S, _ = qkv.shape
    whole = lambda part: pl.BlockSpec((1, S, SB_WIDTH), lambda b, i: (b, 0, part),
                                      pipeline_mode=pl.Buffered(1))
    return pl.pallas_call(
        functools.partial(_attn_kernel, blk=blk),
        grid=(B, S // blk),
        in_specs=[
            pl.BlockSpec((1, blk, SB_WIDTH), lambda b, i: (b, i, 0)),
            whole(1), whole(2),
        ],
        out_specs=pl.BlockSpec((1, blk, SB_WIDTH), lambda b, i: (b, i, 0)),
        out_shape=jax.ShapeDtypeStruct((B, S, SB_WIDTH), BF16),
        scratch_shapes=[pltpu.VMEM((blk, SB_WIDTH), F32),
                        pltpu.VMEM((SB_HEADS * blk, 1), F32)],
        compiler_params=pltpu.CompilerParams(
            dimension_semantics=("arbitrary", "arbitrary"), vmem_limit_bytes=VMEM_LIMIT),
        name="sb_attention",
    )(qkv, qkv, qkv)


def _mixer_tail_kernel(attn_ref, u_ref, uh_ref, g_ref, x_ref,
                       poolw_ref, poolb_ref, pools_ref, wba_ref, wbp_ref, wout_ref,
                       post1_ref, gate1_ref, pre2_ref, sc2_ref, sh2_ref, rw_ref, rb_ref,
                       x1_ref, h2_ref, meta_ref, route_ref, counts_ref, carry_ref, *, tm, tps):
    i = pl.program_id(0)
    D = x_ref.shape[1]
    d_pool = u_ref.shape[1]
    gdim = d_pool // POOL_GROUPS
    seq_tile = i % tps

    u = u_ref[...]
    halo = jnp.where(seq_tile == 0, 0.0, uh_ref[...])
    s = jnp.concatenate([halo, u], axis=0)
    sums = []
    span = 1
    for w in POOL_WINDOWS:
        while span < w:
            s = s + pltpu.roll(s, span, 0)
            span *= 2
        sums.append(s[POOL_HALO:])
    pos1 = (seq_tile * tm + 1 + lax.broadcasted_iota(jnp.int32, (tm, 1), 0)).astype(F32)
    colp = lax.broadcasted_iota(jnp.int32, (1, d_pool), 1)
    mean = sums[-1] / jnp.minimum(pos1, float(POOL_WINDOWS[-1]))
    for gi in range(POOL_GROUPS - 2, -1, -1):
        mean = jnp.where(colp < (gi + 1) * gdim,
                         sums[gi] / jnp.minimum(pos1, float(POOL_WINDOWS[gi])), mean)
    mixed = mean - u
    yp = jnp.dot(mixed.astype(BF16), poolw_ref[...], preferred_element_type=F32)
    pool = ((yp + poolb_ref[...]) * pools_ref[...]).astype(BF16)

    pa = jnp.dot(attn_ref[...], wba_ref[...], preferred_element_type=F32)
    pp = jnp.dot(pool, wbp_ref[...], preferred_element_type=F32)
    merged = g_ref[:, 0:D].astype(F32) * pa + g_ref[:, D:2 * D].astype(F32) * pp
    y = jnp.dot(merged.astype(BF16), wout_ref[...], preferred_element_type=F32)
    x1 = x_ref[...] + gate1_ref[0] * (y * _rms(y) * post1_ref[...])
    x1_ref[...] = x1
    h2 = (x1 * _rms(x1) * pre2_ref[...]) * (1.0 + sc2_ref[0]) + sh2_ref[0]
    h2_ref[...] = h2

    h_hi = h2.astype(BF16)
    h_lo = (h2 - h_hi.astype(F32)).astype(BF16)
    logits = (jnp.dot(h_hi, rw_ref[0], preferred_element_type=F32)
              + (jnp.dot(h_lo, rw_ref[0], preferred_element_type=F32)
                 + jnp.dot(h_hi, rw_ref[1], preferred_element_type=F32))) + rb_ref[...]
    lane = lax.broadcasted_iota(jnp.int32, (tm, LANES), 1).astype(F32)
    work = logits
    vals, idxs, hots = [], [], []
    for _ in range(TOP_K):
        m = jnp.max(work, axis=-1, keepdims=True)
        idx = jnp.min(jnp.where(work == m, lane, float(LANES)), axis=-1, keepdims=True)
        hot = lane == idx
        work = jnp.where(hot, -jnp.inf, work)
        vals.append(m)
        idxs.append(idx)
        hots.append(hot)
    exps = [jnp.exp(vk - vals[0]) for vk in vals]
    denom = exps[0] + exps[1] + exps[2] + exps[3]
    chosen = (hots[0] | hots[1] | hots[2] | hots[3]).astype(F32)

    @pl.when(i == 0)
    def _():
        carry_ref[...] = jnp.zeros_like(carry_ref)

    trow = lax.broadcasted_iota(jnp.int32, (tm, tm), 0)
    tcol = lax.broadcasted_iota(jnp.int32, (tm, tm), 1)
    before = (tcol < trow).astype(BF16)
    rank_all = jnp.dot(before, chosen.astype(BF16), preferred_element_type=F32) + carry_ref[...]
    meta = jnp.zeros((tm, LANES), F32)
    for kk in range(TOP_K):
        rk = jnp.sum(jnp.where(hots[kk], rank_all, 0.0), axis=-1, keepdims=True)
        meta = jnp.where(lane == float(META_E + kk), idxs[kk], meta)
        meta = jnp.where(lane == float(META_G + kk), exps[kk] / denom, meta)
        meta = jnp.where(lane == float(META_R + kk), rk, meta)
    meta_ref[...] = meta
    route_ref[...] = jnp.transpose(meta)[0:ROUTE_ROWS, :]
    carry_ref[...] += jnp.sum(chosen, axis=0, keepdims=True)
    counts_ref[...] = carry_ref[...]


def _mixer_tail(attn, u, g, x2d, poolw_bd, poolb, pools, wba, wbp, wout,
                post1, gate1, pre2, sc2, sh2, rw_pad, rb_pad, *, seq, tm=512):
    T, D = x2d.shape
    d_pool = u.shape[1]
    tps = seq // tm
    hpt = tm // POOL_HALO

    def const(shape):
        return pl.BlockSpec(shape, lambda i: (0,) * len(shape))

    row = lambda w: pl.BlockSpec((tm, w), lambda i: (i, 0))
    vec = pl.BlockSpec((1, 1, D), lambda i: (i // tps, 0, 0))
    return pl.pallas_call(
        functools.partial(_mixer_tail_kernel, tm=tm, tps=tps),
        grid=(T // tm,),
        in_specs=[
            row(SB_WIDTH), row(d_pool),
            pl.BlockSpec((POOL_HALO, d_pool), lambda i: (jnp.maximum(i * hpt - 1, 0), 0)),
            row(2 * D), row(D),
            const((d_pool, d_pool)), const((1, d_pool)), const((1, d_pool)),
            const((SB_WIDTH, D)), const((d_pool, D)), const((D, D)),
            const((1, D)), vec, const((1, D)), vec, vec,
            const((2, D, LANES)), const((1, LANES)),
        ],
        out_specs=[row(D), row(D), row(LANES), pl.BlockSpec((ROUTE_ROWS, tm), lambda i: (0, i)),
                   const((1, LANES))],
        out_shape=[
            jax.ShapeDtypeStruct((T, D), F32),
            jax.ShapeDtypeStruct((T, D), F32),
            jax.ShapeDtypeStruct((T, LANES), F32),
            jax.ShapeDtypeStruct((ROUTE_ROWS, T), F32),
            jax.ShapeDtypeStruct((1, LANES), F32),
        ],
        scratch_shapes=[pltpu.VMEM((1, LANES), F32)],
        compiler_params=pltpu.CompilerParams(
            dimension_semantics=("arbitrary",), vmem_limit_bytes=VMEM_LIMIT),
        name="mixer_tail",
    )(attn, u, u, g, x2d, poolw_bd, poolb, pools, wba, wbp, wout,
      post1, gate1, pre2, sc2, sh2, rw_pad, rb_pad)


def _row_map_kernel(dest_ref, pad_ref, inv_ref, *, chunk, n_pairs):
    i = pl.program_id(0)

    @pl.when(i == 0)
    def _():
        def pad(p, c):
            inv_ref[pad_ref[p]] = n_pairs + p
            return c

        lax.fori_loop(0, pad_ref.shape[0], pad, 0, unroll=8)

    def real(j, c):
        inv_ref[dest_ref[j]] = i * chunk + j
        return c

    lax.fori_loop(0, chunk, real, 0, unroll=8)


def _row_map(dest, pad_rows, *, chunk=1024):
    n_pairs = dest.shape[0]
    n_rows = n_pairs + pad_rows.shape[0]
    return pl.pallas_call(
        functools.partial(_row_map_kernel, chunk=chunk, n_pairs=n_pairs),
        grid=(n_pairs // chunk,),
        in_specs=[
            pl.BlockSpec((chunk,), lambda i: (i,), memory_space=pltpu.SMEM),
            pl.BlockSpec(memory_space=pltpu.SMEM),
        ],
        out_specs=pl.BlockSpec(memory_space=pltpu.SMEM),
        out_shape=jax.ShapeDtypeStruct((n_rows,), jnp.int32),
        compiler_params=pltpu.CompilerParams(dimension_semantics=("arbitrary",)),
        name="moe_row_map",
    )(dest, pad_rows)


def _ffn_kernel(be_ref, slots_a_ref, slots_b_ref, slots_prev_ref, slots_cur_ref,
                toks_0_ref, toks_1_ref, toks_ahead_ref, h_ref,
                wgu_ref, bgu_ref, wdn_ref, bdn_ref, yg_ref,
                wgu_bf, wdn_bf, xbuf0, xbuf1, xbuf2, stage0, stage1, stage2, sem_g, sem_s,
                *, n_blocks):
    b = pl.program_id(0)
    xbuf = (xbuf0, xbuf1, xbuf2)
    stage = (stage0, stage1, stage2)
    depth = len(xbuf)
    rb = xbuf0.shape[0]
    de = wdn_ref.shape[2]
    last = n_blocks - 1

    def gather(toks_ref, s, r):
        return pltpu.make_async_copy(
            h_ref.at[pl.ds(toks_ref[0, 0, r], 1), :], xbuf[s].at[pl.ds(r, 1), :], sem_g.at[s])

    def scatter(slots_ref, s, r):
        return pltpu.make_async_copy(
            stage[s].at[pl.ds(r, 1), :], yg_ref.at[pl.ds(slots_ref[0, 0, r], 1), :], sem_s.at[s])

    def gathered(s):
        return pltpu.make_async_copy(h_ref.at[pl.ds(0, rb), :], xbuf[s], sem_g.at[s])

    def scattered(s):
        return pltpu.make_async_copy(stage[s], yg_ref.at[pl.ds(0, rb), :], sem_s.at[s])

    def start_all(make, ref, s):
        def one(r, c):
            make(ref, s, r).start()
            return c

        lax.fori_loop(0, rb, one, 0, unroll=8)

    @pl.when(b == 0)
    def _():
        for st in stage:
            st[...] = jnp.zeros_like(st)
        start_all(gather, toks_0_ref, 0)
        start_all(gather, toks_1_ref, 1)
        start_all(scatter, slots_a_ref, 0)
        start_all(scatter, slots_b_ref, 1)

    @pl.when((b == 0) | (be_ref[b] != be_ref[jnp.maximum(b - 1, 0)]))
    def _():
        wgu_bf[...] = wgu_ref[0, 0].astype(BF16)
        wdn_bf[...] = wdn_ref[0, 0].astype(BF16)

    def block_step(s):
        gathered(s).wait()
        for r in range(rb):
            gather(toks_ahead_ref, (s + 2) % depth, r).start(priority=r % 2)
            scatter(slots_prev_ref, (s - 1) % depth, r).start(priority=r % 2)
        x = xbuf[s][...].astype(BF16)
        gu = jnp.dot(x, wgu_bf[...], preferred_element_type=F32) + bgu_ref[0, 0]
        gate = jnp.minimum(gu[:, 0:de], SWIGLU_LIMIT)
        up = jnp.clip(gu[:, de:2 * de], -SWIGLU_LIMIT, SWIGLU_LIMIT)
        act = (up + 1.0) * gate * jax.nn.sigmoid(SWIGLU_ALPHA * gate)
        y = jnp.dot(act.astype(BF16), wdn_bf[...], preferred_element_type=F32) + bdn_ref[0, 0]
        scattered(s).wait()
        stage[s][...] = y

    for s in range(depth):
        pl.when(b % depth == s)(functools.partial(block_step, s))

    @pl.when(b == last)
    def _():
        s = last % depth
        start_all(scatter, slots_cur_ref, s)
        for k in range(depth):
            scattered(k).wait()
        gathered((s + 1) % depth).wait()
        gathered((s + 2) % depth).wait()


N_SPARE_BLOCKS = 3


def _ffn(block_e, toks, slots, h2, w_gu, b_gu, w_dn, b_dn, layer):
    T, D = h2.shape
    L, E, _, n_gu = w_gu.shape
    de = w_dn.shape[2]
    n_blocks = toks.shape[0] - N_SPARE_BLOCKS
    n_out = (n_blocks + N_SPARE_BLOCKS) * ROW_BLOCK
    first = N_SPARE_BLOCKS
    assert n_blocks >= 3

    def view(index):
        return pl.BlockSpec((1, 1, ROW_BLOCK), lambda b, be: (index(b), 0, 0),
                            memory_space=pltpu.SMEM)

    grid_spec = pltpu.PrefetchScalarGridSpec(
        num_scalar_prefetch=1,
        grid=(n_blocks,),
        in_specs=[
            view(lambda b: 0), view(lambda b: 1),
            view(lambda b: b + first - 1),
            view(lambda b: b + first),
            view(lambda b: first), view(lambda b: first + 1),
            view(lambda b: jnp.minimum(b + 2, n_blocks - 1) + first),
            pl.BlockSpec(memory_space=pl.ANY),
            pl.BlockSpec((1, 1, D, n_gu), lambda b, be: (layer, be[b], 0, 0)),
            pl.BlockSpec((1, 1, 1, n_gu), lambda b, be: (layer, be[b], 0, 0)),
            pl.BlockSpec((1, 1, de, D), lambda b, be: (layer, be[b], 0, 0)),
            pl.BlockSpec((1, 1, 1, D), lambda b, be: (layer, be[b], 0, 0)),
        ],
        out_specs=pl.BlockSpec(memory_space=pl.ANY),
        scratch_shapes=[pltpu.VMEM((D, n_gu), BF16), pltpu.VMEM((de, D), BF16)]
        + [pltpu.VMEM((ROW_BLOCK, D), F32)] * 6
        + [pltpu.SemaphoreType.DMA((3,)), pltpu.SemaphoreType.DMA((3,))],
    )
    return pl.pallas_call(
        functools.partial(_ffn_kernel, n_blocks=n_blocks),
        grid_spec=grid_spec,
        out_shape=jax.ShapeDtypeStruct((n_out, D), F32),
        compiler_params=pltpu.CompilerParams(
            dimension_semantics=("arbitrary",), vmem_limit_bytes=VMEM_LIMIT),
        name="moe_ffn",
    )(block_e, slots, slots, slots, slots, toks, toks, toks, h2,
      w_gu, b_gu.reshape(L, E, 1, n_gu), w_dn, b_dn.reshape(L, E, 1, D))


def _combine_kernel(*refs):
    yk_refs = refs[:TOP_K]
    meta_ref, x1_ref, post2_ref, gate2_ref, o_ref = refs[TOP_K:]
    y = meta_ref[:, META_G:META_G + 1] * yk_refs[0][...]
    for kk in range(1, TOP_K):
        y = y + meta_ref[:, META_G + kk:META_G + kk + 1] * yk_refs[kk][...]
    o_ref[...] = x1_ref[...] + gate2_ref[0] * (y * _rms(y) * post2_ref[...])


def _combine(yg, meta, x1, post2, gate2, *, seq, tm=256):
    T, D = x1.shape
    tps = seq // tm
    steps = T // tm
    return pl.pallas_call(
        _combine_kernel,
        grid=(steps,),
        in_specs=[pl.BlockSpec((tm, D), functools.partial(lambda i, k: (k * steps + i, 0), k=k))
                  for k in range(TOP_K)] + [
            pl.BlockSpec((tm, LANES), lambda i: (i, 0)),
            pl.BlockSpec((tm, D), lambda i: (i, 0)),
            pl.BlockSpec((1, D), lambda i: (0, 0)),
            pl.BlockSpec((1, 1, D), lambda i: (i // tps, 0, 0)),
        ],
        out_specs=pl.BlockSpec((tm, D), lambda i: (i, 0)),
        out_shape=jax.ShapeDtypeStruct((T, D), F32),
        compiler_params=pltpu.CompilerParams(
            dimension_semantics=("arbitrary",), vmem_limit_bytes=VMEM_LIMIT),
        name="moe_combine",
    )(*([yg] * TOP_K), meta, x1, post2, gate2)


def _block_diag(w):
    g, a, b = w.shape
    out = jnp.zeros((g * a, g * b), w.dtype)
    for i in range(g):
        out = out.at[i * a:(i + 1) * a, i * b:(i + 1) * b].set(w[i])
    return out


def kernel(x, c, ada_w, ada_b, pre1_g, post1_g, pre2_g, post2_g, w_in, pool_w, pool_b, pool_scale,
           w_br_attn, w_br_pool, w_out, router_w, router_b, w_gu, b_gu, w_dn, b_dn):
    B, S, D = x.shape
    L = ada_w.shape[0]
    T = B * S
    d_pool = D - SB_WIDTH
    n_rows = T * TOP_K + N_EXPERTS * ROW_BLOCK
    n_blocks = n_rows // ROW_BLOCK

    c_pad = jnp.pad(c, ((0, 8 - B), (0, 0)))
    mod = _ada(c_pad, ada_w, ada_b)[:, :B]

    x2d = x.reshape(T, D)
    for l in range(L):
        shift1, scale1, gate1, shift2, scale2, gate2 = [
            mod[l, :, i * D:(i + 1) * D].reshape(B, 1, D) for i in range(6)]

        qkv, u, g = _in_proj(x2d, pre1_g[l], scale1, shift1, w_in[l].astype(BF16), seq=S)
        attn = _attention(qkv.reshape(B, S, 3 * SB_WIDTH)).reshape(T, SB_WIDTH)

        rw_pad = jnp.pad(router_w[l], ((0, 0), (0, LANES - N_EXPERTS)))
        rw_hi = rw_pad.astype(BF16)
        rw_pad = jnp.stack([rw_hi, (rw_pad - rw_hi.astype(F32)).astype(BF16)])
        rb_pad = jnp.pad(router_b[l], (0, LANES - N_EXPERTS), constant_values=-jnp.inf)
        x1, h2, meta, route, counts = _mixer_tail(
            attn, u, g, x2d,
            _block_diag(pool_w[l]).astype(BF16), pool_b[l].reshape(1, d_pool),
            pool_scale[l].reshape(1, d_pool),
            w_br_attn[l].astype(BF16), w_br_pool[l].astype(BF16), w_out[l].astype(BF16),
            post1_g[l].reshape(1, D), gate1, pre2_g[l].reshape(1, D), scale2, shift2,
            rw_pad, rb_pad.reshape(1, LANES), seq=S)

        e_idx = route[META_E:META_E + TOP_K].astype(jnp.int32)
        rank = route[META_R:META_R + TOP_K].astype(jnp.int32)
        cnt = counts[0, :N_EXPERTS].astype(jnp.int32)
        padded = ((cnt + ROW_BLOCK - 1) // ROW_BLOCK) * ROW_BLOCK
        pend = jnp.cumsum(padded)
        pstart = pend - padded
        dest = (pstart[e_idx] + rank).reshape(-1)
        blk0 = jnp.arange(n_blocks, dtype=jnp.int32) * ROW_BLOCK
        block_e = jnp.minimum(jnp.sum(pend[None, :] <= blk0[:, None], axis=1), N_EXPERTS - 1).astype(jnp.int32)
        gap = padded - cnt
        gap_end = jnp.cumsum(gap)
        p = jnp.arange(n_rows - T * TOP_K, dtype=jnp.int32)
        pe = jnp.sum(gap_end[None, :] <= p[:, None], axis=1)
        pe_c = jnp.minimum(pe, N_EXPERTS - 1)
        in_expert = pstart[pe_c] + cnt[pe_c] + (p - (gap_end - gap)[pe_c])
        pad_rows = jnp.where(pe < N_EXPERTS, in_expert, pend[-1] + (p - gap_end[-1])).astype(jnp.int32)

        slot = _row_map(dest, pad_rows)
        tok = jnp.where(slot < T * TOP_K, slot % T, 0)
        n_spare = N_SPARE_BLOCKS * ROW_BLOCK
        as_blocks = lambda v: v.reshape(n_blocks + N_SPARE_BLOCKS, 1, ROW_BLOCK)
        toks = as_blocks(jnp.concatenate([jnp.zeros((n_spare,), jnp.int32), tok]))
        slots = as_blocks(jnp.concatenate([n_rows + jnp.arange(n_spare, dtype=jnp.int32), slot]))
        yg = _ffn(block_e, toks, slots, h2, w_gu, b_gu, w_dn, b_dn, l)
        x2d = _combine(yg, meta, x1, post2_g[l].reshape(1, D), gate2, seq=S)
    return x2d.reshape(B, S, D)
```

```python
import functools

import jax
import jax.numpy as jnp
from jax import lax
from jax.experimental import pallas as pl
from jax.experimental.pallas import tpu as pltpu

F32 = jnp.float32
BF16 = jnp.bfloat16

SB_HEADS = 4
SB_HEAD_DIM = 64
SB_WIDTH = SB_HEADS * SB_HEAD_DIM
POOL_WINDOWS = (2, 4, 8, 16)
POOL_GROUPS = len(POOL_WINDOWS)
POOL_HALO = 16
N_EXPERTS = 32
TOP_K = 4
ROW_BLOCK = 256
SWIGLU_LIMIT = 7.0
SWIGLU_ALPHA = 1.702
NORM_EPS = 1e-6
LOG2_E = 1.4426950408889634
F32_UNDERFLOW_LOG2 = -150.0

LANES = 128
VMEM_LIMIT = 52 * 1024 * 1024

META_E, META_G, META_R = 0, 4, 8
ROUTE_ROWS = 16


def _rms(x):
    return lax.rsqrt(jnp.mean(x * x, axis=-1, keepdims=True) + NORM_EPS)


def _ada_kernel(c_ref, w_ref, b_ref, o_ref):
    c = c_ref[...]
    ca = c * jax.nn.sigmoid(c)
    o_ref[0] = jnp.dot(ca, w_ref[0], precision=lax.Precision.HIGHEST,
                       preferred_element_type=F32) + b_ref[0]


def _ada(c_pad, ada_w, ada_b):
    L, D, N = ada_w.shape
    bp = c_pad.shape[0]
    tn = 1536
    return pl.pallas_call(
        _ada_kernel,
        grid=(L, N // tn),
        in_specs=[
            pl.BlockSpec((bp, D), lambda l, j: (0, 0)),
            pl.BlockSpec((1, D, tn), lambda l, j: (l, 0, j)),
            pl.BlockSpec((1, 1, tn), lambda l, j: (l, 0, j)),
        ],
        out_specs=pl.BlockSpec((1, bp, tn), lambda l, j: (l, 0, j)),
        out_shape=jax.ShapeDtypeStruct((L, bp, N), F32),
        compiler_params=pltpu.CompilerParams(
            dimension_semantics=("arbitrary", "arbitrary"), vmem_limit_bytes=VMEM_LIMIT),
        name="ada_mod",
    )(c_pad, ada_w, ada_b.reshape(L, 1, N))


def _in_proj_kernel(x_ref, g_ref, sc_ref, sh_ref, w_ref, qkv_ref, u_ref, gate_ref, *, d_pool):
    x = x_ref[...]
    h = (x * _rms(x) * g_ref[...]) * (1.0 + sc_ref[0]) + sh_ref[0]
    hb = h.astype(BF16)
    n_qkv = 3 * SB_WIDTH
    qkv = jnp.dot(hb, w_ref[:, 0:n_qkv], preferred_element_type=F32)
    col = lax.broadcasted_iota(jnp.int32, (1, n_qkv), 1)
    qkv = jnp.where(col < SB_WIDTH, qkv * (LOG2_E * SB_HEAD_DIM ** -0.5), qkv)
    qkv_ref[...] = qkv.astype(BF16)
    u_ref[...] = jnp.dot(hb, w_ref[:, n_qkv:n_qkv + d_pool], preferred_element_type=F32)
    gl = jnp.dot(hb, w_ref[:, n_qkv + d_pool:], preferred_element_type=F32)
    gate_ref[...] = jax.nn.sigmoid(gl).astype(BF16)


def _in_proj(x2d, g, scale, shift, w_bf, *, seq, tm=512):
    T, D = x2d.shape
    n_in = w_bf.shape[1]
    n_qkv = 3 * SB_WIDTH
    d_pool = D - SB_WIDTH
    n_gate = n_in - n_qkv - d_pool
    tps = seq // tm
    vec = pl.BlockSpec((1, 1, D), lambda i: (i // tps, 0, 0))
    return pl.pallas_call(
        functools.partial(_in_proj_kernel, d_pool=d_pool),
        grid=(T // tm,),
        in_specs=[
            pl.BlockSpec((tm, D), lambda i: (i, 0)),
            pl.BlockSpec((1, D), lambda i: (0, 0)),
            vec, vec,
            pl.BlockSpec((D, n_in), lambda i: (0, 0)),
        ],
        out_specs=[
            pl.BlockSpec((tm, n_qkv), lambda i: (i, 0)),
            pl.BlockSpec((tm, d_pool), lambda i: (i, 0)),
            pl.BlockSpec((tm, n_gate), lambda i: (i, 0)),
        ],
        out_shape=[
            jax.ShapeDtypeStruct((T, n_qkv), BF16),
            jax.ShapeDtypeStruct((T, d_pool), F32),
            jax.ShapeDtypeStruct((T, n_gate), BF16),
        ],
        compiler_params=pltpu.CompilerParams(
            dimension_semantics=("arbitrary",), vmem_limit_bytes=VMEM_LIMIT),
        name="prenorm_in_proj",
    )(x2d, g.reshape(1, D), scale, shift, w_bf)


def _attn_kernel(q_ref, k_ref, v_ref, o_ref, acc_ref, carry_ref, *, blk):
    i = pl.program_id(1)
    width = q_ref.shape[2]
    heads = width // SB_HEAD_DIM
    head_of_lane = lax.broadcasted_iota(jnp.int32, (1, width), 1) // SB_HEAD_DIM
    q = q_ref[0]
    zero = jnp.zeros((), BF16)
    qs = jnp.concatenate([jnp.where(head_of_lane == h, q, zero) for h in range(heads)], axis=0)
    row = lax.broadcasted_iota(jnp.int32, (blk, blk), 0)
    col = lax.broadcasted_iota(jnp.int32, (blk, blk), 1)
    tri = (row > col).astype(BF16)
    causal = jnp.concatenate([col < row] * heads, axis=0)

    def step(j, mask):
        ks = pl.multiple_of(j * blk, blk)
        k = k_ref[0, pl.ds(ks, blk), :]
        v = v_ref[0, pl.ds(ks, blk), :]
        t = lax.dot_general(qs, k, (((1,), (1,)), ((), ())), preferred_element_type=F32)
        soft = jnp.log2(1.0 + jnp.exp2(-jnp.abs(t)))
        log_take = jnp.minimum(t, 0.0) - soft
        log_keep = log_take - t
        if mask is not None:
            log_keep = jnp.where(mask, log_keep, 0.0)
        later = jnp.dot(log_keep.astype(BF16), tri, preferred_element_type=F32)
        a = jnp.exp2(log_take + later + carry_ref[...])
        if mask is not None:
            a = jnp.where(mask, a, 0.0)
        a = a.astype(BF16)
        a_wide = jnp.concatenate([a[h * blk:(h + 1) * blk] for h in range(heads)], axis=1)
        v_heads = jnp.concatenate([jnp.where(head_of_lane == h, v, zero) for h in range(heads)], axis=0)
        acc_ref[...] += jnp.dot(a_wide, v_heads, preferred_element_type=F32)
        carry_ref[...] += jnp.sum(log_keep, axis=-1, keepdims=True)

    acc_ref[...] = jnp.zeros_like(acc_ref)
    carry_ref[...] = jnp.zeros_like(carry_ref)
    step(i, causal)

    def more(state):
        jj, carry_max = state
        return jnp.logical_and(jj < i, carry_max > F32_UNDERFLOW_LOG2)

    def body(state):
        jj, _ = state
        step(i - 1 - jj, None)
        return jj + 1, jnp.max(carry_ref[...])

    lax.while_loop(more, body, (jnp.int32(0), jnp.max(carry_ref[...])))
    o_ref[0] = acc_ref[...].astype(o_ref.dtype)


def _attention(qkv, *, blk=256):
    B, S, _ = qkv.shape
    whole = lambda part: pl.BlockSpec((1, S, SB_WIDTH), lambda b, i: (b, 0, part),
                                      pipeline_mode=pl.Buffered(1))
    return pl.pallas_call(
        functools.partial(_attn_kernel, blk=blk),
        grid=(B, S // blk),
        in_specs=[
            pl.BlockSpec((1, blk, SB_WIDTH), lambda b, i: (b, i, 0)),
            whole(1), whole(2),
        ],
        out_specs=pl.BlockSpec((1, blk, SB_WIDTH), lambda b, i: (b, i, 0)),
        out_shape=jax.ShapeDtypeStruct((B, S, SB_WIDTH), BF16),
        scratch_shapes=[pltpu.VMEM((blk, SB_WIDTH), F32),
                        pltpu.VMEM((SB_HEADS * blk, 1), F32)],
        compiler_params=pltpu.CompilerParams(
            dimension_semantics=("arbitrary", "arbitrary"), vmem_limit_bytes=VMEM_LIMIT),
        name="sb_attention",
    )(qkv, qkv, qkv)


def _mixer_tail_kernel(attn_ref, u_ref, uh_ref, g_ref, x_ref,
                       poolw_ref, poolb_ref, pools_ref, wba_ref, wbp_ref, wout_ref,
                       post1_ref, gate1_ref, pre2_ref, sc2_ref, sh2_ref, rw_ref, rb_ref,
                       x1_ref, h2_ref, meta_ref, route_ref, counts_ref, carry_ref, *, tm, tps):
    i = pl.program_id(0)
    D = x_ref.shape[1]
    d_pool = u_ref.shape[1]
    gdim = d_pool // POOL_GROUPS
    seq_tile = i % tps

    u = u_ref[...]
    halo = jnp.where(seq_tile == 0, 0.0, uh_ref[...])
    s = jnp.concatenate([halo, u], axis=0)
    sums = []
    span = 1
    for w in POOL_WINDOWS:
        while span < w:
            s = s + pltpu.roll(s, span, 0)
            span *= 2
        sums.append(s[POOL_HALO:])
    pos1 = (seq_tile * tm + 1 + lax.broadcasted_iota(jnp.int32, (tm, 1), 0)).astype(F32)
    colp = lax.broadcasted_iota(jnp.int32, (1, d_pool), 1)
    mean = sums[-1] / jnp.minimum(pos1, float(POOL_WINDOWS[-1]))
    for gi in range(POOL_GROUPS - 2, -1, -1):
        mean = jnp.where(colp < (gi + 1) * gdim,
                         sums[gi] / jnp.minimum(pos1, float(POOL_WINDOWS[gi])), mean)
    mixed = mean - u
    yp = jnp.dot(mixed.astype(BF16), poolw_ref[...], preferred_element_type=F32)
    pool = ((yp + poolb_ref[...]) * pools_ref[...]).astype(BF16)

    pa = jnp.dot(attn_ref[...], wba_ref[...], preferred_element_type=F32)
    pp = jnp.dot(pool, wbp_ref[...], preferred_element_type=F32)
    merged = g_ref[:, 0:D].astype(F32) * pa + g_ref[:, D:2 * D].astype(F32) * pp
    y = jnp.dot(merged.astype(BF16), wout_ref[...], preferred_element_type=F32)
    x1 = x_ref[...] + gate1_ref[0] * (y * _rms(y) * post1_ref[...])
    x1_ref[...] = x1
    h2 = (x1 * _rms(x1) * pre2_ref[...]) * (1.0 + sc2_ref[0]) + sh2_ref[0]
    h2_ref[...] = h2

    h_hi = h2.astype(BF16)
    h_lo = (h2 - h_hi.astype(F32)).astype(BF16)
    logits = (jnp.dot(h_hi, rw_ref[0], preferred_element_type=F32)
              + (jnp.dot(h_lo, rw_ref[0], preferred_element_type=F32)
                 + jnp.dot(h_hi, rw_ref[1], preferred_element_type=F32))) + rb_ref[...]
    lane = lax.broadcasted_iota(jnp.int32, (tm, LANES), 1).astype(F32)
    work = logits
    vals, idxs, hots = [], [], []
    for _ in range(TOP_K):
        m = jnp.max(work, axis=-1, keepdims=True)
        idx = jnp.min(jnp.where(work == m, lane, float(LANES)), axis=-1, keepdims=True)
        hot = lane == idx
        work = jnp.where(hot, -jnp.inf, work)
        vals.append(m)
        idxs.append(idx)
        hots.append(hot)
    exps = [jnp.exp(vk - vals[0]) for vk in vals]
    denom = exps[0] + exps[1] + exps[2] + exps[3]
    chosen = (hots[0] | hots[1] | hots[2] | hots[3]).astype(F32)

    @pl.when(i == 0)
    def _():
        carry_ref[...] = jnp.zeros_like(carry_ref)

    trow = lax.broadcasted_iota(jnp.int32, (tm, tm), 0)
    tcol = lax.broadcasted_iota(jnp.int32, (tm, tm), 1)
    before = (tcol < trow).astype(BF16)
    rank_all = jnp.dot(before, chosen.astype(BF16), preferred_element_type=F32) + carry_ref[...]
    meta = jnp.zeros((tm, LANES), F32)
    for kk in range(TOP_K):
        rk = jnp.sum(jnp.where(hots[kk], rank_all, 0.0), axis=-1, keepdims=True)
        meta = jnp.where(lane == float(META_E + kk), idxs[kk], meta)
        meta = jnp.where(lane == float(META_G + kk), exps[kk] / denom, meta)
        meta = jnp.where(lane == float(META_R + kk), rk, meta)
    meta_ref[...] = meta
    route_ref[...] = jnp.transpose(meta)[0:ROUTE_ROWS, :]
    carry_ref[...] += jnp.sum(chosen, axis=0, keepdims=True)
    counts_ref[...] = carry_ref[...]


def _mixer_tail(attn, u, g, x2d, poolw_bd, poolb, pools, wba, wbp, wout,
                post1, gate1, pre2, sc2, sh2, rw_pad, rb_pad, *, seq, tm=512):
    T, D = x2d.shape
    d_pool = u.shape[1]
    tps = seq // tm
    hpt = tm // POOL_HALO

    def const(shape):
        return pl.BlockSpec(shape, lambda i: (0,) * len(shape))

    row = lambda w: pl.BlockSpec((tm, w), lambda i: (i, 0))
    vec = pl.BlockSpec((1, 1, D), lambda i: (i // tps, 0, 0))
    return pl.pallas_call(
        functools.partial(_mixer_tail_kernel, tm=tm, tps=tps),
        grid=(T // tm,),
        in_specs=[
            row(SB_WIDTH), row(d_pool),
            pl.BlockSpec((POOL_HALO, d_pool), lambda i: (jnp.maximum(i * hpt - 1, 0), 0)),
            row(2 * D), row(D),
            const((d_pool, d_pool)), const((1, d_pool)), const((1, d_pool)),
            const((SB_WIDTH, D)), const((d_pool, D)), const((D, D)),
            const((1, D)), vec, const((1, D)), vec, vec,
            const((2, D, LANES)), const((1, LANES)),
        ],
        out_specs=[row(D), row(D), row(LANES), pl.BlockSpec((ROUTE_ROWS, tm), lambda i: (0, i)),
                   const((1, LANES))],
        out_shape=[
            jax.ShapeDtypeStruct((T, D), F32),
            jax.ShapeDtypeStruct((T, D), F32),
            jax.ShapeDtypeStruct((T, LANES), F32),
            jax.ShapeDtypeStruct((ROUTE_ROWS, T), F32),
            jax.ShapeDtypeStruct((1, LANES), F32),
        ],
        scratch_shapes=[pltpu.VMEM((1, LANES), F32)],
        compiler_params=pltpu.CompilerParams(
            dimension_semantics=("arbitrary",), vmem_limit_bytes=VMEM_LIMIT),
        name="mixer_tail",
    )(attn, u, u, g, x2d, poolw_bd, poolb, pools, wba, wbp, wout,
      post1, gate1, pre2, sc2, sh2, rw_pad, rb_pad)


def _row_map_kernel(dest_ref, pad_ref, inv_ref, *, chunk, n_pairs):
    i = pl.program_id(0)

    @pl.when(i == 0)
    def _():
        def pad(p, c):
            inv_ref[pad_ref[p]] = n_pairs + p
            return c

        lax.fori_loop(0, pad_ref.shape[0], pad, 0, unroll=8)

    def real(j, c):
        inv_ref[dest_ref[j]] = i * chunk + j
        return c

    lax.fori_loop(0, chunk, real, 0, unroll=8)


def _row_map(dest, pad_rows, *, chunk=1024):
    n_pairs = dest.shape[0]
    n_rows = n_pairs + pad_rows.shape[0]
    return pl.pallas_call(
        functools.partial(_row_map_kernel, chunk=chunk, n_pairs=n_pairs),
        grid=(n_pairs // chunk,),
        in_specs=[
            pl.BlockSpec((chunk,), lambda i: (i,), memory_space=pltpu.SMEM),
            pl.BlockSpec(memory_space=pltpu.SMEM),
        ],
        out_specs=pl.BlockSpec(memory_space=pltpu.SMEM),
        out_shape=jax.ShapeDtypeStruct((n_rows,), jnp.int32),
        compiler_params=pltpu.CompilerParams(dimension_semantics=("arbitrary",)),
        name="moe_row_map",
    )(dest, pad_rows)


def _ffn_kernel(be_ref, slots_a_ref, slots_b_ref, slots_prev_ref, slots_cur_ref,
                toks_0_ref, toks_1_ref, toks_ahead_ref, h_ref,
                wgu_ref, bgu_ref, wdn_ref, bdn_ref, yg_ref,
                wgu_bf, wdn_bf, xbuf0, xbuf1, xbuf2, stage0, stage1, stage2, sem_g, sem_s,
                *, n_blocks):
    b = pl.program_id(0)
    xbuf = (xbuf0, xbuf1, xbuf2)
    stage = (stage0, stage1, stage2)
    depth = len(xbuf)
    rb = xbuf0.shape[0]
    de = wdn_ref.shape[2]
    last = n_blocks - 1

    def gather(toks_ref, s, r):
        return pltpu.make_async_copy(
            h_ref.at[pl.ds(toks_ref[0, 0, r], 1), :], xbuf[s].at[pl.ds(r, 1), :], sem_g.at[s])

    def scatter(slots_ref, s, r):
        return pltpu.make_async_copy(
            stage[s].at[pl.ds(r, 1), :], yg_ref.at[pl.ds(slots_ref[0, 0, r], 1), :], sem_s.at[s])

    def gathered(s):
        return pltpu.make_async_copy(h_ref.at[pl.ds(0, rb), :], xbuf[s], sem_g.at[s])

    def scattered(s):
        return pltpu.make_async_copy(stage[s], yg_ref.at[pl.ds(0, rb), :], sem_s.at[s])

    def start_all(make, ref, s):
        def one(r, c):
            make(ref, s, r).start()
            return c

        lax.fori_loop(0, rb, one, 0, unroll=8)

    @pl.when(b == 0)
    def _():
        for st in stage:
            st[...] = jnp.zeros_like(st)
        start_all(gather, toks_0_ref, 0)
        start_all(gather, toks_1_ref, 1)
        start_all(scatter, slots_a_ref, 0)
        start_all(scatter, slots_b_ref, 1)

    @pl.when((b == 0) | (be_ref[b] != be_ref[jnp.maximum(b - 1, 0)]))
    def _():
        wgu_bf[...] = wgu_ref[0, 0].astype(BF16)
        wdn_bf[...] = wdn_ref[0, 0].astype(BF16)

    def block_step(s):
        gathered(s).wait()
        for r in range(rb):
            gather(toks_ahead_ref, (s + 2) % depth, r).start(priority=r % 2)
            scatter(slots_prev_ref, (s - 1) % depth, r).start(priority=r % 2)
        x = xbuf[s][...].astype(BF16)
        gu = jnp.dot(x, wgu_bf[...], preferred_element_type=F32) + bgu_ref[0, 0]
        gate = jnp.minimum(gu[:, 0:de], SWIGLU_LIMIT)
        up = jnp.clip(gu[:, de:2 * de], -SWIGLU_LIMIT, SWIGLU_LIMIT)
        act = (up + 1.0) * gate * jax.nn.sigmoid(SWIGLU_ALPHA * gate)
        y = jnp.dot(act.astype(BF16), wdn_bf[...], preferred_element_type=F32) + bdn_ref[0, 0]
        scattered(s).wait()
        stage[s][...] = y

    for s in range(depth):
        pl.when(b % depth == s)(functools.partial(block_step, s))

    @pl.when(b == last)
    def _():
        s = last % depth
        start_all(scatter, slots_cur_ref, s)
        for k in range(depth):
            scattered(k).wait()
        gathered((s + 1) % depth).wait()
        gathered((s + 2) % depth).wait()


N_SPARE_BLOCKS = 3


def _ffn(block_e, toks, slots, h2, w_gu, b_gu, w_dn, b_dn, layer):
    T, D = h2.shape
    L, E, _, n_gu = w_gu.shape
    de = w_dn.shape[2]
    n_blocks = toks.shape[0] - N_SPARE_BLOCKS
    n_out = (n_blocks + N_SPARE_BLOCKS) * ROW_BLOCK
    first = N_SPARE_BLOCKS
    assert n_blocks >= 3

    def view(index):
        return pl.BlockSpec((1, 1, ROW_BLOCK), lambda b, be: (index(b), 0, 0),
                            memory_space=pltpu.SMEM)

    grid_spec = pltpu.PrefetchScalarGridSpec(
        num_scalar_prefetch=1,
        grid=(n_blocks,),
        in_specs=[
            view(lambda b: 0), view(lambda b: 1),
            view(lambda b: b + first - 1),
            view(lambda b: b + first),
            view(lambda b: first), view(lambda b: first + 1),
            view(lambda b: jnp.minimum(b + 2, n_blocks - 1) + first),
            pl.BlockSpec(memory_space=pl.ANY),
            pl.BlockSpec((1, 1, D, n_gu), lambda b, be: (layer, be[b], 0, 0)),
            pl.BlockSpec((1, 1, 1, n_gu), lambda b, be: (layer, be[b], 0, 0)),
            pl.BlockSpec((1, 1, de, D), lambda b, be: (layer, be[b], 0, 0)),
            pl.BlockSpec((1, 1, 1, D), lambda b, be: (layer, be[b], 0, 0)),
        ],
        out_specs=pl.BlockSpec(memory_space=pl.ANY),
        scratch_shapes=[pltpu.VMEM((D, n_gu), BF16), pltpu.VMEM((de, D), BF16)]
        + [pltpu.VMEM((ROW_BLOCK, D), F32)] * 6
        + [pltpu.SemaphoreType.DMA((3,)), pltpu.SemaphoreType.DMA((3,))],
    )
    return pl.pallas_call(
        functools.partial(_ffn_kernel, n_blocks=n_blocks),
        grid_spec=grid_spec,
        out_shape=jax.ShapeDtypeStruct((n_out, D), F32),
        compiler_params=pltpu.CompilerParams(
            dimension_semantics=("arbitrary",), vmem_limit_bytes=VMEM_LIMIT),
        name="moe_ffn",
    )(block_e, slots, slots, slots, slots, toks, toks, toks, h2,
      w_gu, b_gu.reshape(L, E, 1, n_gu), w_dn, b_dn.reshape(L, E, 1, D))


def _combine_kernel(*refs):
    yk_refs = refs[:TOP_K]
    meta_ref, x1_ref, post2_ref, gate2_ref, o_ref = refs[TOP_K:]
    y = meta_ref[:, META_G:META_G + 1] * yk_refs[0][...]
    for kk in range(1, TOP_K):
        y = y + meta_ref[:, META_G + kk:META_G + kk + 1] * yk_refs[kk][...]
    o_ref[...] = x1_ref[...] + gate2_ref[0] * (y * _rms(y) * post2_ref[...])


def _combine(yg, meta, x1, post2, gate2, *, seq, tm=256):
    T, D = x1.shape
    tps = seq // tm
    steps = T // tm
    return pl.pallas_call(
        _combine_kernel,
        grid=(steps,),
        in_specs=[pl.BlockSpec((tm, D), functools.partial(lambda i, k: (k * steps + i, 0), k=k))
                  for k in range(TOP_K)] + [
            pl.BlockSpec((tm, LANES), lambda i: (i, 0)),
            pl.BlockSpec((tm, D), lambda i: (i, 0)),
            pl.BlockSpec((1, D), lambda i: (0, 0)),
            pl.BlockSpec((1, 1, D), lambda i: (i // tps, 0, 0)),
        ],
        out_specs=pl.BlockSpec((tm, D), lambda i: (i, 0)),
        out_shape=jax.ShapeDtypeStruct((T, D), F32),
        compiler_params=pltpu.CompilerParams(
            dimension_semantics=("arbitrary",), vmem_limit_bytes=VMEM_LIMIT),
        name="moe_combine",
    )(*([yg] * TOP_K), meta, x1, post2, gate2)


def _block_diag(w):
    g, a, b = w.shape
    out = jnp.zeros((g * a, g * b), w.dtype)
    for i in range(g):
        out = out.at[i * a:(i + 1) * a, i * b:(i + 1) * b].set(w[i])
    return out


def kernel(x, c, ada_w, ada_b, pre1_g, post1_g, pre2_g, post2_g, w_in, pool_w, pool_b, pool_scale,
           w_br_attn, w_br_pool, w_out, router_w, router_b, w_gu, b_gu, w_dn, b_dn):
    B, S, D = x.shape
    L = ada_w.shape[0]
    T = B * S
    d_pool = D - SB_WIDTH
    n_rows = T * TOP_K + N_EXPERTS * ROW_BLOCK
    n_blocks = n_rows // ROW_BLOCK

    c_pad = jnp.pad(c, ((0, 8 - B), (0, 0)))
    mod = _ada(c_pad, ada_w, ada_b)[:, :B]

    x2d = x.reshape(T, D)
    for l in range(L):
        shift1, scale1, gate1, shift2, scale2, gate2 = [
            mod[l, :, i * D:(i + 1) * D].reshape(B, 1, D) for i in range(6)]

        qkv, u, g = _in_proj(x2d, pre1_g[l], scale1, shift1, w_in[l].astype(BF16), seq=S)
        attn = _attention(qkv.reshape(B, S, 3 * SB_WIDTH)).reshape(T, SB_WIDTH)

        rw_pad = jnp.pad(router_w[l], ((0, 0), (0, LANES - N_EXPERTS)))
        rw_hi = rw_pad.astype(BF16)
        rw_pad = jnp.stack([rw_hi, (rw_pad - rw_hi.astype(F32)).astype(BF16)])
        rb_pad = jnp.pad(router_b[l], (0, LANES - N_EXPERTS), constant_values=-jnp.inf)
        x1, h2, meta, route, counts = _mixer_tail(
            attn, u, g, x2d,
            _block_diag(pool_w[l]).astype(BF16), pool_b[l].reshape(1, d_pool),
            pool_scale[l].reshape(1, d_pool),
            w_br_attn[l].astype(BF16), w_br_pool[l].astype(BF16), w_out[l].astype(BF16),
            post1_g[l].reshape(1, D), gate1, pre2_g[l].reshape(1, D), scale2, shift2,
            rw_pad, rb_pad.reshape(1, LANES), seq=S)

        e_idx = route[META_E:META_E + TOP_K].astype(jnp.int32)
        rank = route[META_R:META_R + TOP_K].astype(jnp.int32)
        cnt = counts[0, :N_EXPERTS].astype(jnp.int32)
        padded = ((cnt + ROW_BLOCK - 1) // ROW_BLOCK) * ROW_BLOCK
        pend = jnp.cumsum(padded)
        pstart = pend - padded
        experts = jnp.arange(N_EXPERTS, dtype=jnp.int32)[:, None, None]
        first_row = jnp.sum(jnp.where(e_idx[None] == experts, pstart[:, None, None], 0), axis=0)
        dest = (first_row + rank).reshape(-1)
        blk0 = jnp.arange(n_blocks, dtype=jnp.int32) * ROW_BLOCK
        block_e = jnp.minimum(jnp.sum(pend[None, :] <= blk0[:, None], axis=1), N_EXPERTS - 1).astype(jnp.int32)
        gap = padded - cnt
        gap_end = jnp.cumsum(gap)
        p = jnp.arange(n_rows - T * TOP_K, dtype=jnp.int32)
        pe = jnp.sum(gap_end[None, :] <= p[:, None], axis=1)
        pe_c = jnp.minimum(pe, N_EXPERTS - 1)
        in_expert = pstart[pe_c] + cnt[pe_c] + (p - (gap_end - gap)[pe_c])
        pad_rows = jnp.where(pe < N_EXPERTS, in_expert, pend[-1] + (p - gap_end[-1])).astype(jnp.int32)

        slot = _row_map(dest, pad_rows)
        tok = jnp.where(slot < T * TOP_K, slot % T, 0)
        n_spare = N_SPARE_BLOCKS * ROW_BLOCK
        as_blocks = lambda v: v.reshape(n_blocks + N_SPARE_BLOCKS, 1, ROW_BLOCK)
        toks = as_blocks(jnp.concatenate([jnp.zeros((n_spare,), jnp.int32), tok]))
        slots = as_blocks(jnp.concatenate([n_rows + jnp.arange(n_spare, dtype=jnp.int32), slot]))
        yg = _ffn(block_e, toks, slots, h2, w_gu, b_gu, w_dn, b_dn, l)
        x2d = _combine(yg, meta, x1, post2_g[l].reshape(1, D), gate2, seq=S)
    return x2d.reshape(B, S, D)
```

```python
import functools

import jax
import jax.numpy as jnp
from jax import lax
from jax.experimental import pallas as pl
from jax.experimental.pallas import tpu as pltpu

F32 = jnp.float32
BF16 = jnp.bfloat16

SB_HEADS = 4
SB_HEAD_DIM = 64
SB_WIDTH = SB_HEADS * SB_HEAD_DIM
POOL_WINDOWS = (2, 4, 8, 16)
POOL_GROUPS = len(POOL_WINDOWS)
POOL_HALO = 16
N_EXPERTS = 32
TOP_K = 4
ROW_BLOCK = 256
SWIGLU_LIMIT = 7.0
SWIGLU_ALPHA = 1.702
NORM_EPS = 1e-6
LOG2_E = 1.4426950408889634
F32_UNDERFLOW_LOG2 = -150.0

LANES = 128
VMEM_LIMIT = 52 * 1024 * 1024

META_E, META_G, META_R = 0, 4, 8
ROUTE_ROWS = 16


def _rms(x):
    return lax.rsqrt(jnp.mean(x * x, axis=-1, keepdims=True) + NORM_EPS)


def _ada_kernel(c_ref, w_ref, b_ref, o_ref):
    c = c_ref[...]
    ca = c * jax.nn.sigmoid(c)
    o_ref[0] = jnp.dot(ca, w_ref[0], precision=lax.Precision.HIGHEST,
                       preferred_element_type=F32) + b_ref[0]


def _ada(c_pad, ada_w, ada_b):
    L, D, N = ada_w.shape
    bp = c_pad.shape[0]
    tn = 1536
    return pl.pallas_call(
        _ada_kernel,
        grid=(L, N // tn),
        in_specs=[
            pl.BlockSpec((bp, D), lambda l, j: (0, 0)),
            pl.BlockSpec((1, D, tn), lambda l, j: (l, 0, j)),
            pl.BlockSpec((1, 1, tn), lambda l, j: (l, 0, j)),
        ],
        out_specs=pl.BlockSpec((1, bp, tn), lambda l, j: (l, 0, j)),
        out_shape=jax.ShapeDtypeStruct((L, bp, N), F32),
        compiler_params=pltpu.CompilerParams(
            dimension_semantics=("arbitrary", "arbitrary"), vmem_limit_bytes=VMEM_LIMIT),
        name="ada_mod",
    )(c_pad, ada_w, ada_b.reshape(L, 1, N))


def _in_proj_kernel(x_ref, g_ref, sc_ref, sh_ref, w_ref, qkv_ref, u_ref, gate_ref, *, d_pool):
    x = x_ref[...]
    h = (x * _rms(x) * g_ref[...]) * (1.0 + sc_ref[0]) + sh_ref[0]
    hb = h.astype(BF16)
    n_qkv = 3 * SB_WIDTH
    qkv = jnp.dot(hb, w_ref[:, 0:n_qkv], preferred_element_type=F32)
    col = lax.broadcasted_iota(jnp.int32, (1, n_qkv), 1)
    qkv = jnp.where(col < SB_WIDTH, qkv * (LOG2_E * SB_HEAD_DIM ** -0.5), qkv)
    qkv_ref[...] = qkv.astype(BF16)
    u_ref[...] = jnp.dot(hb, w_ref[:, n_qkv:n_qkv + d_pool], preferred_element_type=F32)
    gl = jnp.dot(hb, w_ref[:, n_qkv + d_pool:], preferred_element_type=F32)
    gate_ref[...] = jax.nn.sigmoid(gl).astype(BF16)


def _in_proj(x2d, g, scale, shift, w_bf, *, seq, tm=512):
    T, D = x2d.shape
    n_in = w_bf.shape[1]
    n_qkv = 3 * SB_WIDTH
    d_pool = D - SB_WIDTH
    n_gate = n_in - n_qkv - d_pool
    tps = seq // tm
    vec = pl.BlockSpec((1, 1, D), lambda i: (i // tps, 0, 0))
    return pl.pallas_call(
        functools.partial(_in_proj_kernel, d_pool=d_pool),
        grid=(T // tm,),
        in_specs=[
            pl.BlockSpec((tm, D), lambda i: (i, 0)),
            pl.BlockSpec((1, D), lambda i: (0, 0)),
            vec, vec,
            pl.BlockSpec((D, n_in), lambda i: (0, 0)),
        ],
        out_specs=[
            pl.BlockSpec((tm, n_qkv), lambda i: (i, 0)),
            pl.BlockSpec((tm, d_pool), lambda i: (i, 0)),
            pl.BlockSpec((tm, n_gate), lambda i: (i, 0)),
        ],
        out_shape=[
            jax.ShapeDtypeStruct((T, n_qkv), BF16),
            jax.ShapeDtypeStruct((T, d_pool), F32),
            jax.ShapeDtypeStruct((T, n_gate), BF16),
        ],
        compiler_params=pltpu.CompilerParams(
            dimension_semantics=("arbitrary",), vmem_limit_bytes=VMEM_LIMIT),
        name="prenorm_in_proj",
    )(x2d, g.reshape(1, D), scale, shift, w_bf)


def _attn_kernel(q_ref, k_ref, v_ref, o_ref, acc_ref, carry_ref, *, blk):
    i = pl.program_id(1)
    width = q_ref.shape[2]
    heads = width // SB_HEAD_DIM
    head_of_lane = lax.broadcasted_iota(jnp.int32, (1, width), 1) // SB_HEAD_DIM
    q = q_ref[0]
    zero = jnp.zeros((), BF16)
    qs = jnp.concatenate([jnp.where(head_of_lane == h, q, zero) for h in range(heads)], axis=0)
    row = lax.broadcasted_iota(jnp.int32, (blk, blk), 0)
    col = lax.broadcasted_iota(jnp.int32, (blk, blk), 1)
    tri = (row > col).astype(BF16)
    causal = jnp.concatenate([col < row] * heads, axis=0)

    def step(j, mask):
        ks = pl.multiple_of(j * blk, blk)
        k = k_ref[0, pl.ds(ks, blk), :]
        v = v_ref[0, pl.ds(ks, blk), :]
        t = lax.dot_general(qs, k, (((1,), (1,)), ((), ())), preferred_element_type=F32)
        soft = jnp.log2(1.0 + jnp.exp2(-jnp.abs(t)))
        log_take = jnp.minimum(t, 0.0) - soft
        log_keep = log_take - t
        if mask is not None:
            log_keep = jnp.where(mask, log_keep, 0.0)
        later = jnp.dot(log_keep.astype(BF16), tri, preferred_element_type=F32)
        a = jnp.exp2(log_take + later + carry_ref[...])
        if mask is not None:
            a = jnp.where(mask, a, 0.0)
        a = a.astype(BF16)
        a_wide = jnp.concatenate([a[h * blk:(h + 1) * blk] for h in range(heads)], axis=1)
        v_heads = jnp.concatenate([jnp.where(head_of_lane == h, v, zero) for h in range(heads)], axis=0)
        acc_ref[...] += jnp.dot(a_wide, v_heads, preferred_element_type=F32)
        carry_ref[...] += jnp.sum(log_keep, axis=-1, keepdims=True)

    acc_ref[...] = jnp.zeros_like(acc_ref)
    carry_ref[...] = jnp.zeros_like(carry_ref)
    step(i, causal)

    def more(state):
        jj, carry_max = state
        return jnp.logical_and(jj < i, carry_max > F32_UNDERFLOW_LOG2)

    def body(state):
        jj, _ = state
        step(i - 1 - jj, None)
        return jj + 1, jnp.max(carry_ref[...])

    lax.while_loop(more, body, (jnp.int32(0), jnp.max(carry_ref[...])))
    o_ref[0] = acc_ref[...].astype(o_ref.dtype)


def _attention(qkv, *, blk=256):
    B, S, _ = qkv.shape
    whole = lambda part: pl.BlockSpec((1, S, SB_WIDTH), lambda b, i: (b, 0, part),
                                      pipeline_mode=pl.Buffered(1))
    return pl.pallas_call(
        functools.partial(_attn_kernel, blk=blk),
        grid=(B, S // blk),
        in_specs=[
            pl.BlockSpec((1, blk, SB_WIDTH), lambda b, i: (b, i, 0)),
            whole(1), whole(2),
        ],
        out_specs=pl.BlockSpec((1, blk, SB_WIDTH), lambda b, i: (b, i, 0)),
        out_shape=jax.ShapeDtypeStruct((B, S, SB_WIDTH), BF16),
        scratch_shapes=[pltpu.VMEM((blk, SB_WIDTH), F32),
                        pltpu.VMEM((SB_HEADS * blk, 1), F32)],
        compiler_params=pltpu.CompilerParams(
            dimension_semantics=("arbitrary", "arbitrary"), vmem_limit_bytes=VMEM_LIMIT),
        name="sb_attention",
    )(qkv, qkv, qkv)


def _mixer_tail_kernel(attn_ref, u_ref, uh_ref, g_ref, x_ref,
                       poolw_ref, poolb_ref, pools_ref, wba_ref, wbp_ref, wout_ref,
                       post1_ref, gate1_ref, pre2_ref, sc2_ref, sh2_ref, rw_ref, rb_ref,
                       x1_ref, h2_ref, meta_ref, route_ref, counts_ref, carry_ref, *, tm, tps):
    i = pl.program_id(0)
    D = x_ref.shape[1]
    d_pool = u_ref.shape[1]
    gdim = d_pool // POOL_GROUPS
    seq_tile = i % tps

    u = u_ref[...]
    halo = jnp.where(seq_tile == 0, 0.0, uh_ref[...])
    s = jnp.concatenate([halo, u], axis=0)
    sums = []
    span = 1
    for w in POOL_WINDOWS:
        while span < w:
            s = s + pltpu.roll(s, span, 0)
            span *= 2
        sums.append(s[POOL_HALO:])
    pos1 = (seq_tile * tm + 1 + lax.broadcasted_iota(jnp.int32, (tm, 1), 0)).astype(F32)
    colp = lax.broadcasted_iota(jnp.int32, (1, d_pool), 1)
    mean = sums[-1] / jnp.minimum(pos1, float(POOL_WINDOWS[-1]))
    for gi in range(POOL_GROUPS - 2, -1, -1):
        mean = jnp.where(colp < (gi + 1) * gdim,
                         sums[gi] / jnp.minimum(pos1, float(POOL_WINDOWS[gi])), mean)
    mixed = mean - u
    yp = jnp.dot(mixed.astype(BF16), poolw_ref[...], preferred_element_type=F32)
    pool = ((yp + poolb_ref[...]) * pools_ref[...]).astype(BF16)

    pa = jnp.dot(attn_ref[...], wba_ref[...], preferred_element_type=F32)
    pp = jnp.dot(pool, wbp_ref[...], preferred_element_type=F32)
    merged = g_ref[:, 0:D].astype(F32) * pa + g_ref[:, D:2 * D].astype(F32) * pp
    y = jnp.dot(merged.astype(BF16), wout_ref[...], preferred_element_type=F32)
    x1 = x_ref[...] + gate1_ref[0] * (y * _rms(y) * post1_ref[...])
    x1_ref[...] = x1
    h2 = (x1 * _rms(x1) * pre2_ref[...]) * (1.0 + sc2_ref[0]) + sh2_ref[0]
    h2_ref[...] = h2

    h_hi = h2.astype(BF16)
    h_lo = (h2 - h_hi.astype(F32)).astype(BF16)
    logits = (jnp.dot(h_hi, rw_ref[0], preferred_element_type=F32)
              + (jnp.dot(h_lo, rw_ref[0], preferred_element_type=F32)
                 + jnp.dot(h_hi, rw_ref[1], preferred_element_type=F32))) + rb_ref[...]
    lane = lax.broadcasted_iota(jnp.int32, (tm, LANES), 1).astype(F32)
    work = logits
    vals, idxs, hots = [], [], []
    for _ in range(TOP_K):
        m = jnp.max(work, axis=-1, keepdims=True)
        idx = jnp.min(jnp.where(work == m, lane, float(LANES)), axis=-1, keepdims=True)
        hot = lane == idx
        work = jnp.where(hot, -jnp.inf, work)
        vals.append(m)
        idxs.append(idx)
        hots.append(hot)
    exps = [jnp.exp(vk - vals[0]) for vk in vals]
    denom = exps[0] + exps[1] + exps[2] + exps[3]
    chosen = (hots[0] | hots[1] | hots[2] | hots[3]).astype(F32)

    @pl.when(i == 0)
    def _():
        carry_ref[...] = jnp.zeros_like(carry_ref)

    trow = lax.broadcasted_iota(jnp.int32, (tm, tm), 0)
    tcol = lax.broadcasted_iota(jnp.int32, (tm, tm), 1)
    before = (tcol < trow).astype(BF16)
    rank_all = jnp.dot(before, chosen.astype(BF16), preferred_element_type=F32) + carry_ref[...]
    meta = jnp.zeros((tm, LANES), F32)
    for kk in range(TOP_K):
        rk = jnp.sum(jnp.where(hots[kk], rank_all, 0.0), axis=-1, keepdims=True)
        meta = jnp.where(lane == float(META_E + kk), idxs[kk], meta)
        meta = jnp.where(lane == float(META_G + kk), exps[kk] / denom, meta)
        meta = jnp.where(lane == float(META_R + kk), rk, meta)
    meta_ref[...] = meta
    route_ref[...] = jnp.transpose(meta)[0:ROUTE_ROWS, :]
    carry_ref[...] += jnp.sum(chosen, axis=0, keepdims=True)
    counts_ref[...] = carry_ref[...]


def _mixer_tail(attn, u, g, x2d, poolw_bd, poolb, pools, wba, wbp, wout,
                post1, gate1, pre2, sc2, sh2, rw_pad, rb_pad, *, seq, tm=512):
    T, D = x2d.shape
    d_pool = u.shape[1]
    tps = seq // tm
    hpt = tm // POOL_HALO

    def const(shape):
        return pl.BlockSpec(shape, lambda i: (0,) * len(shape))

    row = lambda w: pl.BlockSpec((tm, w), lambda i: (i, 0))
    vec = pl.BlockSpec((1, 1, D), lambda i: (i // tps, 0, 0))
    return pl.pallas_call(
        functools.partial(_mixer_tail_kernel, tm=tm, tps=tps),
        grid=(T // tm,),
        in_specs=[
            row(SB_WIDTH), row(d_pool),
            pl.BlockSpec((POOL_HALO, d_pool), lambda i: (jnp.maximum(i * hpt - 1, 0), 0)),
            row(2 * D), row(D),
            const((d_pool, d_pool)), const((1, d_pool)), const((1, d_pool)),
            const((SB_WIDTH, D)), const((d_pool, D)), const((D, D)),
            const((1, D)), vec, const((1, D)), vec, vec,
            const((2, D, LANES)), const((1, LANES)),
        ],
        out_specs=[row(D), row(D), row(LANES), pl.BlockSpec((ROUTE_ROWS, tm), lambda i: (0, i)),
                   const((1, LANES))],
        out_shape=[
            jax.ShapeDtypeStruct((T, D), F32),
            jax.ShapeDtypeStruct((T, D), F32),
            jax.ShapeDtypeStruct((T, LANES), F32),
            jax.ShapeDtypeStruct((ROUTE_ROWS, T), F32),
            jax.ShapeDtypeStruct((1, LANES), F32),
        ],
        scratch_shapes=[pltpu.VMEM((1, LANES), F32)],
        compiler_params=pltpu.CompilerParams(
            dimension_semantics=("arbitrary",), vmem_limit_bytes=VMEM_LIMIT),
        name="mixer_tail",
    )(attn, u, u, g, x2d, poolw_bd, poolb, pools, wba, wbp, wout,
      post1, gate1, pre2, sc2, sh2, rw_pad, rb_pad)


def _row_map_kernel(dest_ref, pad_ref, inv_ref, *, chunk, n_pairs):
    i = pl.program_id(0)

    @pl.when(i == 0)
    def _():
        def pad(p, c):
            inv_ref[pad_ref[p]] = n_pairs + p
            return c

        lax.fori_loop(0, pad_ref.shape[0], pad, 0, unroll=8)

    def real(j, c):
        inv_ref[dest_ref[j]] = i * chunk + j
        return c

    lax.fori_loop(0, chunk, real, 0, unroll=8)


def _row_map(dest, pad_rows, *, chunk=1024):
    n_pairs = dest.shape[0]
    n_rows = n_pairs + pad_rows.shape[0]
    return pl.pallas_call(
        functools.partial(_row_map_kernel, chunk=chunk, n_pairs=n_pairs),
        grid=(n_pairs // chunk,),
        in_specs=[
            pl.BlockSpec((chunk,), lambda i: (i,), memory_space=pltpu.SMEM),
            pl.BlockSpec(memory_space=pltpu.SMEM),
        ],
        out_specs=pl.BlockSpec(memory_space=pltpu.SMEM),
        out_shape=jax.ShapeDtypeStruct((n_rows,), jnp.int32),
        compiler_params=pltpu.CompilerParams(dimension_semantics=("arbitrary",)),
        name="moe_row_map",
    )(dest, pad_rows)


def _ffn_kernel(be_ref, slots_a_ref, slots_b_ref, slots_prev_ref, slots_cur_ref,
                toks_0_ref, toks_1_ref, toks_ahead_ref, h_ref,
                wgu_ref, bgu_ref, wdn_ref, bdn_ref, yg_ref,
                wgu_bf, wdn_bf, xbuf0, xbuf1, xbuf2, stage0, stage1, stage2, sem_g, sem_s,
                *, n_blocks):
    b = pl.program_id(0)
    xbuf = (xbuf0, xbuf1, xbuf2)
    stage = (stage0, stage1, stage2)
    depth = len(xbuf)
    rb = xbuf0.shape[0]
    de = wdn_ref.shape[2]
    last = n_blocks - 1

    def gather(toks_ref, s, r):
        return pltpu.make_async_copy(
            h_ref.at[pl.ds(toks_ref[0, 0, r], 1), :], xbuf[s].at[pl.ds(r, 1), :], sem_g.at[s])

    def scatter(slots_ref, s, r):
        return pltpu.make_async_copy(
            stage[s].at[pl.ds(r, 1), :], yg_ref.at[pl.ds(slots_ref[0, 0, r], 1), :], sem_s.at[s])

    def gathered(s):
        return pltpu.make_async_copy(h_ref.at[pl.ds(0, rb), :], xbuf[s], sem_g.at[s])

    def scattered(s):
        return pltpu.make_async_copy(stage[s], yg_ref.at[pl.ds(0, rb), :], sem_s.at[s])

    def start_all(make, ref, s):
        def one(r, c):
            make(ref, s, r).start()
            return c

        lax.fori_loop(0, rb, one, 0, unroll=8)

    @pl.when(b == 0)
    def _():
        for st in stage:
            st[...] = jnp.zeros_like(st)
        start_all(gather, toks_0_ref, 0)
        start_all(gather, toks_1_ref, 1)
        start_all(scatter, slots_a_ref, 0)
        start_all(scatter, slots_b_ref, 1)

    @pl.when((b == 0) | (be_ref[b] != be_ref[jnp.maximum(b - 1, 0)]))
    def _():
        wgu_bf[...] = wgu_ref[0, 0].astype(BF16)
        wdn_bf[...] = wdn_ref[0, 0].astype(BF16)

    def block_step(s):
        gathered(s).wait()
        for r in range(rb):
            gather(toks_ahead_ref, (s + 2) % depth, r).start(priority=r % 2)
            scatter(slots_prev_ref, (s - 1) % depth, r).start(priority=r % 2)
        x = xbuf[s][...].astype(BF16)
        gu = jnp.dot(x, wgu_bf[...], preferred_element_type=F32) + bgu_ref[0, 0]
        gate = jnp.minimum(gu[:, 0:de], SWIGLU_LIMIT)
        up = jnp.clip(gu[:, de:2 * de], -SWIGLU_LIMIT, SWIGLU_LIMIT)
        act = (up + 1.0) * gate * jax.nn.sigmoid(SWIGLU_ALPHA * gate)
        y = jnp.dot(act.astype(BF16), wdn_bf[...], preferred_element_type=F32) + bdn_ref[0, 0]
        scattered(s).wait()
        stage[s][...] = y

    def idle_step(s):
        gathered(s).wait()
        start_all(gather, toks_ahead_ref, (s + 2) % depth)
        start_all(scatter, slots_prev_ref, (s - 1) % depth)
        scattered(s).wait()
        stage[s][...] = jnp.zeros_like(stage[s])

    n_used = be_ref[n_blocks]
    for s in range(depth):
        pl.when((b % depth == s) & (b < n_used))(functools.partial(block_step, s))
        pl.when((b % depth == s) & (b >= n_used))(functools.partial(idle_step, s))

    @pl.when(b == last)
    def _():
        s = last % depth
        start_all(scatter, slots_cur_ref, s)
        for k in range(depth):
            scattered(k).wait()
        gathered((s + 1) % depth).wait()
        gathered((s + 2) % depth).wait()


N_SPARE_BLOCKS = 3


def _ffn(block_e, toks, slots, h2, w_gu, b_gu, w_dn, b_dn, layer):
    T, D = h2.shape
    L, E, _, n_gu = w_gu.shape
    de = w_dn.shape[2]
    n_blocks = toks.shape[0] - N_SPARE_BLOCKS
    n_out = (n_blocks + N_SPARE_BLOCKS) * ROW_BLOCK
    first = N_SPARE_BLOCKS
    assert n_blocks >= 3

    def view(index):
        return pl.BlockSpec((1, 1, ROW_BLOCK), lambda b, be: (index(b), 0, 0),
                            memory_space=pltpu.SMEM)

    grid_spec = pltpu.PrefetchScalarGridSpec(
        num_scalar_prefetch=1,
        grid=(n_blocks,),
        in_specs=[
            view(lambda b: 0), view(lambda b: 1),
            view(lambda b: b + first - 1),
            view(lambda b: b + first),
            view(lambda b: first), view(lambda b: first + 1),
            view(lambda b: jnp.minimum(b + 2, n_blocks - 1) + first),
            pl.BlockSpec(memory_space=pl.ANY),
            pl.BlockSpec((1, 1, D, n_gu), lambda b, be: (layer, be[b], 0, 0)),
            pl.BlockSpec((1, 1, 1, n_gu), lambda b, be: (layer, be[b], 0, 0)),
            pl.BlockSpec((1, 1, de, D), lambda b, be: (layer, be[b], 0, 0)),
            pl.BlockSpec((1, 1, 1, D), lambda b, be: (layer, be[b], 0, 0)),
        ],
        out_specs=pl.BlockSpec(memory_space=pl.ANY),
        scratch_shapes=[pltpu.VMEM((D, n_gu), BF16), pltpu.VMEM((de, D), BF16)]
        + [pltpu.VMEM((ROW_BLOCK, D), F32)] * 6
        + [pltpu.SemaphoreType.DMA((3,)), pltpu.SemaphoreType.DMA((3,))],
    )
    return pl.pallas_call(
        functools.partial(_ffn_kernel, n_blocks=n_blocks),
        grid_spec=grid_spec,
        out_shape=jax.ShapeDtypeStruct((n_out, D), F32),
        compiler_params=pltpu.CompilerParams(
            dimension_semantics=("arbitrary",), vmem_limit_bytes=VMEM_LIMIT),
        name="moe_ffn",
    )(block_e, slots, slots, slots, slots, toks, toks, toks, h2,
      w_gu, b_gu.reshape(L, E, 1, n_gu), w_dn, b_dn.reshape(L, E, 1, D))


def _combine_kernel(*refs):
    yk_refs = refs[:TOP_K]
    meta_ref, x1_ref, post2_ref, gate2_ref, o_ref = refs[TOP_K:]
    y = meta_ref[:, META_G:META_G + 1] * yk_refs[0][...]
    for kk in range(1, TOP_K):
        y = y + meta_ref[:, META_G + kk:META_G + kk + 1] * yk_refs[kk][...]
    o_ref[...] = x1_ref[...] + gate2_ref[0] * (y * _rms(y) * post2_ref[...])


def _combine(yg, meta, x1, post2, gate2, *, seq, tm=256):
    T, D = x1.shape
    tps = seq // tm
    steps = T // tm
    return pl.pallas_call(
        _combine_kernel,
        grid=(steps,),
        in_specs=[pl.BlockSpec((tm, D), functools.partial(lambda i, k: (k * steps + i, 0), k=k))
                  for k in range(TOP_K)] + [
            pl.BlockSpec((tm, LANES), lambda i: (i, 0)),
            pl.BlockSpec((tm, D), lambda i: (i, 0)),
            pl.BlockSpec((1, D), lambda i: (0, 0)),
            pl.BlockSpec((1, 1, D), lambda i: (i // tps, 0, 0)),
        ],
        out_specs=pl.BlockSpec((tm, D), lambda i: (i, 0)),
        out_shape=jax.ShapeDtypeStruct((T, D), F32),
        compiler_params=pltpu.CompilerParams(
            dimension_semantics=("arbitrary",), vmem_limit_bytes=VMEM_LIMIT),
        name="moe_combine",
    )(*([yg] * TOP_K), meta, x1, post2, gate2)


def _block_diag(w):
    g, a, b = w.shape
    out = jnp.zeros((g * a, g * b), w.dtype)
    for i in range(g):
        out = out.at[i * a:(i + 1) * a, i * b:(i + 1) * b].set(w[i])
    return out


def kernel(x, c, ada_w, ada_b, pre1_g, post1_g, pre2_g, post2_g, w_in, pool_w, pool_b, pool_scale,
           w_br_attn, w_br_pool, w_out, router_w, router_b, w_gu, b_gu, w_dn, b_dn):
    B, S, D = x.shape
    L = ada_w.shape[0]
    T = B * S
    d_pool = D - SB_WIDTH
    n_rows = T * TOP_K + N_EXPERTS * ROW_BLOCK
    n_blocks = n_rows // ROW_BLOCK

    c_pad = jnp.pad(c, ((0, 8 - B), (0, 0)))
    mod = _ada(c_pad, ada_w, ada_b)[:, :B]

    x2d = x.reshape(T, D)
    for l in range(L):
        shift1, scale1, gate1, shift2, scale2, gate2 = [
            mod[l, :, i * D:(i + 1) * D].reshape(B, 1, D) for i in range(6)]

        qkv, u, g = _in_proj(x2d, pre1_g[l], scale1, shift1, w_in[l].astype(BF16), seq=S)
        attn = _attention(qkv.reshape(B, S, 3 * SB_WIDTH)).reshape(T, SB_WIDTH)

        rw_pad = jnp.pad(router_w[l], ((0, 0), (0, LANES - N_EXPERTS)))
        rw_hi = rw_pad.astype(BF16)
        rw_pad = jnp.stack([rw_hi, (rw_pad - rw_hi.astype(F32)).astype(BF16)])
        rb_pad = jnp.pad(router_b[l], (0, LANES - N_EXPERTS), constant_values=-jnp.inf)
        x1, h2, meta, route, counts = _mixer_tail(
            attn, u, g, x2d,
            _block_diag(pool_w[l]).astype(BF16), pool_b[l].reshape(1, d_pool),
            pool_scale[l].reshape(1, d_pool),
            w_br_attn[l].astype(BF16), w_br_pool[l].astype(BF16), w_out[l].astype(BF16),
            post1_g[l].reshape(1, D), gate1, pre2_g[l].reshape(1, D), scale2, shift2,
            rw_pad, rb_pad.reshape(1, LANES), seq=S)

        e_idx = route[META_E:META_E + TOP_K].astype(jnp.int32)
        rank = route[META_R:META_R + TOP_K].astype(jnp.int32)
        cnt = counts[0, :N_EXPERTS].astype(jnp.int32)
        padded = ((cnt + ROW_BLOCK - 1) // ROW_BLOCK) * ROW_BLOCK
        pend = jnp.cumsum(padded)
        pstart = pend - padded
        experts = jnp.arange(N_EXPERTS, dtype=jnp.int32)[:, None, None]
        first_row = jnp.sum(jnp.where(e_idx[None] == experts, pstart[:, None, None], 0), axis=0)
        dest = (first_row + rank).reshape(-1)
        blk0 = jnp.arange(n_blocks, dtype=jnp.int32) * ROW_BLOCK
        block_e = jnp.minimum(jnp.sum(pend[None, :] <= blk0[:, None], axis=1), N_EXPERTS - 1).astype(jnp.int32)
        block_e = jnp.concatenate([block_e, pend[-1:] // ROW_BLOCK])
        gap = padded - cnt
        gap_end = jnp.cumsum(gap)
        p = jnp.arange(n_rows - T * TOP_K, dtype=jnp.int32)
        pe = jnp.sum(gap_end[None, :] <= p[:, None], axis=1)
        pe_c = jnp.minimum(pe, N_EXPERTS - 1)
        in_expert = pstart[pe_c] + cnt[pe_c] + (p - (gap_end - gap)[pe_c])
        pad_rows = jnp.where(pe < N_EXPERTS, in_expert, pend[-1] + (p - gap_end[-1])).astype(jnp.int32)

        slot = _row_map(dest, pad_rows)
        tok = jnp.where(slot < T * TOP_K, slot % T, 0)
        n_spare = N_SPARE_BLOCKS * ROW_BLOCK
        as_blocks = lambda v: v.reshape(n_blocks + N_SPARE_BLOCKS, 1, ROW_BLOCK)
        toks = as_blocks(jnp.concatenate([jnp.zeros((n_spare,), jnp.int32), tok]))
        slots = as_blocks(jnp.concatenate([n_rows + jnp.arange(n_spare, dtype=jnp.int32), slot]))
        yg = _ffn(block_e, toks, slots, h2, w_gu, b_gu, w_dn, b_dn, l)
        x2d = _combine(yg, meta, x1, post2_g[l].reshape(1, D), gate2, seq=S)
    return x2d.reshape(B, S, D)
```

```python
import functools

import jax
import jax.numpy as jnp
from jax import lax
from jax.experimental import pallas as pl
from jax.experimental.pallas import tpu as pltpu

F32 = jnp.float32
BF16 = jnp.bfloat16

SB_HEADS = 4
SB_HEAD_DIM = 64
SB_WIDTH = SB_HEADS * SB_HEAD_DIM
POOL_WINDOWS = (2, 4, 8, 16)
POOL_GROUPS = len(POOL_WINDOWS)
POOL_HALO = 16
N_EXPERTS = 32
TOP_K = 4
ROW_BLOCK = 256
SWIGLU_LIMIT = 7.0
SWIGLU_ALPHA = 1.702
NORM_EPS = 1e-6
LOG2_E = 1.4426950408889634
F32_UNDERFLOW_LOG2 = -150.0

LANES = 128
VMEM_LIMIT = 52 * 1024 * 1024

META_E, META_G, META_R = 0, 4, 8
ROUTE_ROWS = 16


def _rms(x):
    return lax.rsqrt(jnp.mean(x * x, axis=-1, keepdims=True) + NORM_EPS)


def _ada_kernel(c_ref, w_ref, b_ref, o_ref):
    c = c_ref[...]
    ca = c * jax.nn.sigmoid(c)
    o_ref[0] = jnp.dot(ca, w_ref[0], precision=lax.Precision.HIGHEST,
                       preferred_element_type=F32) + b_ref[0]


def _ada(c_pad, ada_w, ada_b):
    L, D, N = ada_w.shape
    bp = c_pad.shape[0]
    tn = 1536
    return pl.pallas_call(
        _ada_kernel,
        grid=(L, N // tn),
        in_specs=[
            pl.BlockSpec((bp, D), lambda l, j: (0, 0)),
            pl.BlockSpec((1, D, tn), lambda l, j: (l, 0, j)),
            pl.BlockSpec((1, 1, tn), lambda l, j: (l, 0, j)),
        ],
        out_specs=pl.BlockSpec((1, bp, tn), lambda l, j: (l, 0, j)),
        out_shape=jax.ShapeDtypeStruct((L, bp, N), F32),
        compiler_params=pltpu.CompilerParams(
            dimension_semantics=("arbitrary", "arbitrary"), vmem_limit_bytes=VMEM_LIMIT),
        name="ada_mod",
    )(c_pad, ada_w, ada_b.reshape(L, 1, N))


def _in_proj_kernel(x_ref, g_ref, sc_ref, sh_ref, w_ref, qkv_ref, u_ref, gate_ref, *, d_pool):
    x = x_ref[...]
    h = (x * _rms(x) * g_ref[...]) * (1.0 + sc_ref[0]) + sh_ref[0]
    hb = h.astype(BF16)
    n_qkv = 3 * SB_WIDTH
    qkv = jnp.dot(hb, w_ref[:, 0:n_qkv], preferred_element_type=F32)
    col = lax.broadcasted_iota(jnp.int32, (1, n_qkv), 1)
    qkv = jnp.where(col < SB_WIDTH, qkv * (LOG2_E * SB_HEAD_DIM ** -0.5), qkv)
    qkv_ref[...] = qkv.astype(BF16)
    u_ref[...] = jnp.dot(hb, w_ref[:, n_qkv:n_qkv + d_pool], preferred_element_type=F32)
    gl = jnp.dot(hb, w_ref[:, n_qkv + d_pool:], preferred_element_type=F32)
    gate_ref[...] = jax.nn.sigmoid(gl).astype(BF16)


def _in_proj(x2d, g, scale, shift, w_bf, *, seq, tm=512):
    T, D = x2d.shape
    n_in = w_bf.shape[1]
    n_qkv = 3 * SB_WIDTH
    d_pool = D - SB_WIDTH
    n_gate = n_in - n_qkv - d_pool
    tps = seq // tm
    vec = pl.BlockSpec((1, 1, D), lambda i: (i // tps, 0, 0))
    return pl.pallas_call(
        functools.partial(_in_proj_kernel, d_pool=d_pool),
        grid=(T // tm,),
        in_specs=[
            pl.BlockSpec((tm, D), lambda i: (i, 0)),
            pl.BlockSpec((1, D), lambda i: (0, 0)),
            vec, vec,
            pl.BlockSpec((D, n_in), lambda i: (0, 0)),
        ],
        out_specs=[
            pl.BlockSpec((tm, n_qkv), lambda i: (i, 0)),
            pl.BlockSpec((tm, d_pool), lambda i: (i, 0)),
            pl.BlockSpec((tm, n_gate), lambda i: (i, 0)),
        ],
        out_shape=[
            jax.ShapeDtypeStruct((T, n_qkv), BF16),
            jax.ShapeDtypeStruct((T, d_pool), F32),
            jax.ShapeDtypeStruct((T, n_gate), BF16),
        ],
        compiler_params=pltpu.CompilerParams(
            dimension_semantics=("arbitrary",), vmem_limit_bytes=VMEM_LIMIT),
        name="prenorm_in_proj",
    )(x2d, g.reshape(1, D), scale, shift, w_bf)


def _attn_kernel(q_ref, k_ref, v_ref, o_ref, acc_ref, carry_ref, *, blk):
    i = pl.program_id(1)
    width = q_ref.shape[2]
    heads = width // SB_HEAD_DIM
    head_of_lane = lax.broadcasted_iota(jnp.int32, (1, width), 1) // SB_HEAD_DIM
    q = q_ref[0]
    zero = jnp.zeros((), BF16)
    qs = jnp.concatenate([jnp.where(head_of_lane == h, q, zero) for h in range(heads)], axis=0)
    row = lax.broadcasted_iota(jnp.int32, (blk, blk), 0)
    col = lax.broadcasted_iota(jnp.int32, (blk, blk), 1)
    tri = (row > col).astype(BF16)
    causal = jnp.concatenate([col < row] * heads, axis=0)

    def step(j, mask):
        ks = pl.multiple_of(j * blk, blk)
        k = k_ref[0, pl.ds(ks, blk), :]
        v = v_ref[0, pl.ds(ks, blk), :]
        t = lax.dot_general(qs, k, (((1,), (1,)), ((), ())), preferred_element_type=F32)
        soft = jnp.log2(1.0 + jnp.exp2(-jnp.abs(t)))
        log_take = jnp.minimum(t, 0.0) - soft
        log_keep = log_take - t
        if mask is not None:
            log_keep = jnp.where(mask, log_keep, 0.0)
        later = jnp.dot(log_keep.astype(BF16), tri, preferred_element_type=F32)
        a = jnp.exp2(log_take + later + carry_ref[...])
        if mask is not None:
            a = jnp.where(mask, a, 0.0)
        a = a.astype(BF16)
        a_wide = jnp.concatenate([a[h * blk:(h + 1) * blk] for h in range(heads)], axis=1)
        v_heads = jnp.concatenate([jnp.where(head_of_lane == h, v, zero) for h in range(heads)], axis=0)
        acc_ref[...] += jnp.dot(a_wide, v_heads, preferred_element_type=F32)
        carry_ref[...] += jnp.sum(log_keep, axis=-1, keepdims=True)

    acc_ref[...] = jnp.zeros_like(acc_ref)
    carry_ref[...] = jnp.zeros_like(carry_ref)
    step(i, causal)

    def more(state):
        jj, carry_max = state
        return jnp.logical_and(jj < i, carry_max > F32_UNDERFLOW_LOG2)

    def body(state):
        jj, _ = state
        step(i - 1 - jj, None)
        return jj + 1, jnp.max(carry_ref[...])

    lax.while_loop(more, body, (jnp.int32(0), jnp.max(carry_ref[...])))
    o_ref[0] = acc_ref[...].astype(o_ref.dtype)


def _attention(qkv, *, blk=256):
    B, S, _ = qkv.shape
    whole = lambda part: pl.BlockSpec((1, S, SB_WIDTH), lambda b, i: (b, 0, part),
                                      pipeline_mode=pl.Buffered(1))
    return pl.pallas_call(
        functools.partial(_attn_kernel, blk=blk),
        grid=(B, S // blk),
        in_specs=[
            pl.BlockSpec((1, blk, SB_WIDTH), lambda b, i: (b, i, 0)),
            whole(1), whole(2),
        ],
        out_specs=pl.BlockSpec((1, blk, SB_WIDTH), lambda b, i: (b, i, 0)),
        out_shape=jax.ShapeDtypeStruct((B, S, SB_WIDTH), BF16),
        scratch_shapes=[pltpu.VMEM((blk, SB_WIDTH), F32),
                        pltpu.VMEM((SB_HEADS * blk, 1), F32)],
        compiler_params=pltpu.CompilerParams(
            dimension_semantics=("arbitrary", "arbitrary"), vmem_limit_bytes=VMEM_LIMIT),
        name="sb_attention",
    )(qkv, qkv, qkv)


def _mixer_tail_kernel(attn_ref, u_ref, uh_ref, g_ref, x_ref,
                       poolw_ref, poolb_ref, pools_ref, wba_ref, wbp_ref, wout_ref,
                       post1_ref, gate1_ref, pre2_ref, sc2_ref, sh2_ref, rw_ref, rb_ref,
                       x1_ref, h2_ref, meta_ref, route_ref, counts_ref, carry_ref, *, tm, tps):
    i = pl.program_id(0)
    D = x_ref.shape[1]
    d_pool = u_ref.shape[1]
    gdim = d_pool // POOL_GROUPS
    seq_tile = i % tps

    u = u_ref[...]
    halo = jnp.where(seq_tile == 0, 0.0, uh_ref[...])
    s = jnp.concatenate([halo, u], axis=0)
    sums = []
    span = 1
    for w in POOL_WINDOWS:
        while span < w:
            s = s + pltpu.roll(s, span, 0)
            span *= 2
        sums.append(s[POOL_HALO:])
    pos1 = (seq_tile * tm + 1 + lax.broadcasted_iota(jnp.int32, (tm, 1), 0)).astype(F32)
    colp = lax.broadcasted_iota(jnp.int32, (1, d_pool), 1)
    mean = sums[-1] / jnp.minimum(pos1, float(POOL_WINDOWS[-1]))
    for gi in range(POOL_GROUPS - 2, -1, -1):
        mean = jnp.where(colp < (gi + 1) * gdim,
                         sums[gi] / jnp.minimum(pos1, float(POOL_WINDOWS[gi])), mean)
    mixed = mean - u
    yp = jnp.dot(mixed.astype(BF16), poolw_ref[...], preferred_element_type=F32)
    pool = ((yp + poolb_ref[...]) * pools_ref[...]).astype(BF16)

    pa = jnp.dot(attn_ref[...], wba_ref[...], preferred_element_type=F32)
    pp = jnp.dot(pool, wbp_ref[...], preferred_element_type=F32)
    merged = g_ref[:, 0:D].astype(F32) * pa + g_ref[:, D:2 * D].astype(F32) * pp
    y = jnp.dot(merged.astype(BF16), wout_ref[...], preferred_element_type=F32)
    x1 = x_ref[...] + gate1_ref[0] * (y * _rms(y) * post1_ref[...])
    x1_ref[...] = x1
    h2 = (x1 * _rms(x1) * pre2_ref[...]) * (1.0 + sc2_ref[0]) + sh2_ref[0]
    h2_ref[...] = h2

    h_hi = h2.astype(BF16)
    h_lo = (h2 - h_hi.astype(F32)).astype(BF16)
    logits = (jnp.dot(h_hi, rw_ref[0], preferred_element_type=F32)
              + (jnp.dot(h_lo, rw_ref[0], preferred_element_type=F32)
                 + jnp.dot(h_hi, rw_ref[1], preferred_element_type=F32))) + rb_ref[...]
    lane = lax.broadcasted_iota(jnp.int32, (tm, LANES), 1).astype(F32)
    work = logits
    vals, idxs, hots = [], [], []
    for _ in range(TOP_K):
        m = jnp.max(work, axis=-1, keepdims=True)
        idx = jnp.min(jnp.where(work == m, lane, float(LANES)), axis=-1, keepdims=True)
        hot = lane == idx
        work = jnp.where(hot, -jnp.inf, work)
        vals.append(m)
        idxs.append(idx)
        hots.append(hot)
    exps = [jnp.exp(vk - vals[0]) for vk in vals]
    denom = exps[0] + exps[1] + exps[2] + exps[3]
    chosen = (hots[0] | hots[1] | hots[2] | hots[3]).astype(F32)

    @pl.when(i == 0)
    def _():
        carry_ref[...] = jnp.zeros_like(carry_ref)

    trow = lax.broadcasted_iota(jnp.int32, (tm, tm), 0)
    tcol = lax.broadcasted_iota(jnp.int32, (tm, tm), 1)
    before = (tcol < trow).astype(BF16)
    rank_all = jnp.dot(before, chosen.astype(BF16), preferred_element_type=F32) + carry_ref[...]
    meta = jnp.zeros((tm, LANES), F32)
    for kk in range(TOP_K):
        rk = jnp.sum(jnp.where(hots[kk], rank_all, 0.0), axis=-1, keepdims=True)
        meta = jnp.where(lane == float(META_E + kk), idxs[kk], meta)
        meta = jnp.where(lane == float(META_G + kk), exps[kk] / denom, meta)
        meta = jnp.where(lane == float(META_R + kk), rk, meta)
    meta_ref[...] = meta
    route_ref[...] = jnp.transpose(meta)[0:ROUTE_ROWS, :]
    carry_ref[...] += jnp.sum(chosen, axis=0, keepdims=True)
    counts_ref[...] = carry_ref[...]


def _mixer_tail(attn, u, g, x2d, poolw_bd, poolb, pools, wba, wbp, wout,
                post1, gate1, pre2, sc2, sh2, rw_pad, rb_pad, *, seq, tm=512):
    T, D = x2d.shape
    d_pool = u.shape[1]
    tps = seq // tm
    hpt = tm // POOL_HALO

    def const(shape):
        return pl.BlockSpec(shape, lambda i: (0,) * len(shape))

    row = lambda w: pl.BlockSpec((tm, w), lambda i: (i, 0))
    vec = pl.BlockSpec((1, 1, D), lambda i: (i // tps, 0, 0))
    return pl.pallas_call(
        functools.partial(_mixer_tail_kernel, tm=tm, tps=tps),
        grid=(T // tm,),
        in_specs=[
            row(SB_WIDTH), row(d_pool),
            pl.BlockSpec((POOL_HALO, d_pool), lambda i: (jnp.maximum(i * hpt - 1, 0), 0)),
            row(2 * D), row(D),
            const((d_pool, d_pool)), const((1, d_pool)), const((1, d_pool)),
            const((SB_WIDTH, D)), const((d_pool, D)), const((D, D)),
            const((1, D)), vec, const((1, D)), vec, vec,
            const((2, D, LANES)), const((1, LANES)),
        ],
        out_specs=[row(D), row(D), row(LANES), pl.BlockSpec((ROUTE_ROWS, tm), lambda i: (0, i)),
                   const((1, LANES))],
        out_shape=[
            jax.ShapeDtypeStruct((T, D), F32),
            jax.ShapeDtypeStruct((T, D), F32),
            jax.ShapeDtypeStruct((T, LANES), F32),
            jax.ShapeDtypeStruct((ROUTE_ROWS, T), F32),
            jax.ShapeDtypeStruct((1, LANES), F32),
        ],
        scratch_shapes=[pltpu.VMEM((1, LANES), F32)],
        compiler_params=pltpu.CompilerParams(
            dimension_semantics=("arbitrary",), vmem_limit_bytes=VMEM_LIMIT),
        name="mixer_tail",
    )(attn, u, u, g, x2d, poolw_bd, poolb, pools, wba, wbp, wout,
      post1, gate1, pre2, sc2, sh2, rw_pad, rb_pad)


def _row_map_kernel(dest_ref, pad_ref, inv_ref, *, chunk, n_pairs):
    i = pl.program_id(0)

    @pl.when(i == 0)
    def _():
        def pad(p, c):
            inv_ref[pad_ref[p]] = n_pairs + p
            return c

        lax.fori_loop(0, pad_ref.shape[0], pad, 0, unroll=8)

    def real(j, c):
        inv_ref[dest_ref[j]] = i * chunk + j
        return c

    lax.fori_loop(0, chunk, real, 0, unroll=8)


def _row_map(dest, pad_rows, *, chunk=4096):
    n_pairs = dest.shape[0]
    n_rows = n_pairs + pad_rows.shape[0]
    return pl.pallas_call(
        functools.partial(_row_map_kernel, chunk=chunk, n_pairs=n_pairs),
        grid=(n_pairs // chunk,),
        in_specs=[
            pl.BlockSpec((chunk,), lambda i: (i,), memory_space=pltpu.SMEM),
            pl.BlockSpec(memory_space=pltpu.SMEM),
        ],
        out_specs=pl.BlockSpec(memory_space=pltpu.SMEM),
        out_shape=jax.ShapeDtypeStruct((n_rows,), jnp.int32),
        compiler_params=pltpu.CompilerParams(dimension_semantics=("arbitrary",)),
        name="moe_row_map",
    )(dest, pad_rows)


def _ffn_kernel(be_ref, slots_a_ref, slots_b_ref, slots_prev_ref, slots_cur_ref,
                toks_0_ref, toks_1_ref, toks_ahead_ref, h_ref,
                wgu_ref, bgu_ref, wdn_ref, bdn_ref, yg_ref,
                wgu_bf, wdn_bf, xbuf0, xbuf1, xbuf2, stage0, stage1, stage2, sem_g, sem_s,
                *, n_blocks):
    b = pl.program_id(0)
    xbuf = (xbuf0, xbuf1, xbuf2)
    stage = (stage0, stage1, stage2)
    depth = len(xbuf)
    rb = xbuf0.shape[0]
    de = wdn_ref.shape[2]
    last = n_blocks - 1

    def gather(toks_ref, s, r):
        return pltpu.make_async_copy(
            h_ref.at[pl.ds(toks_ref[0, 0, r], 1), :], xbuf[s].at[pl.ds(r, 1), :], sem_g.at[s])

    def scatter(slots_ref, s, r):
        return pltpu.make_async_copy(
            stage[s].at[pl.ds(r, 1), :], yg_ref.at[pl.ds(slots_ref[0, 0, r], 1), :], sem_s.at[s])

    def gathered(s):
        return pltpu.make_async_copy(h_ref.at[pl.ds(0, rb), :], xbuf[s], sem_g.at[s])

    def scattered(s):
        return pltpu.make_async_copy(stage[s], yg_ref.at[pl.ds(0, rb), :], sem_s.at[s])

    def start_all(make, ref, s):
        def one(r, c):
            make(ref, s, r).start()
            return c

        lax.fori_loop(0, rb, one, 0, unroll=8)

    @pl.when(b == 0)
    def _():
        for st in stage:
            st[...] = jnp.zeros_like(st)
        start_all(gather, toks_0_ref, 0)
        start_all(gather, toks_1_ref, 1)
        start_all(scatter, slots_a_ref, 0)
        start_all(scatter, slots_b_ref, 1)

    @pl.when((b == 0) | (be_ref[b] != be_ref[jnp.maximum(b - 1, 0)]))
    def _():
        wgu_bf[...] = wgu_ref[0, 0].astype(BF16)
        wdn_bf[...] = wdn_ref[0, 0].astype(BF16)

    def block_step(s):
        gathered(s).wait()
        for r in range(rb):
            gather(toks_ahead_ref, (s + 2) % depth, r).start(priority=r % 2)
            scatter(slots_prev_ref, (s - 1) % depth, r).start(priority=r % 2)
        x = xbuf[s][...].astype(BF16)
        gu = jnp.dot(x, wgu_bf[...], preferred_element_type=F32) + bgu_ref[0, 0]
        gate = jnp.minimum(gu[:, 0:de], SWIGLU_LIMIT)
        up = jnp.clip(gu[:, de:2 * de], -SWIGLU_LIMIT, SWIGLU_LIMIT)
        act = (up + 1.0) * gate * jax.nn.sigmoid(SWIGLU_ALPHA * gate)
        y = jnp.dot(act.astype(BF16), wdn_bf[...], preferred_element_type=F32) + bdn_ref[0, 0]
        scattered(s).wait()
        stage[s][...] = y

    for s in range(depth):
        pl.when(b % depth == s)(functools.partial(block_step, s))

    @pl.when(b == last)
    def _():
        s = last % depth
        start_all(scatter, slots_cur_ref, s)
        for k in range(depth):
            scattered(k).wait()
        gathered((s + 1) % depth).wait()
        gathered((s + 2) % depth).wait()


N_SPARE_BLOCKS = 3


def _ffn(block_e, toks, slots, h2, w_gu, b_gu, w_dn, b_dn, layer):
    T, D = h2.shape
    L, E, _, n_gu = w_gu.shape
    de = w_dn.shape[2]
    n_blocks = toks.shape[0] - N_SPARE_BLOCKS
    n_out = (n_blocks + N_SPARE_BLOCKS) * ROW_BLOCK
    first = N_SPARE_BLOCKS
    assert n_blocks >= 3

    def view(index):
        return pl.BlockSpec((1, 1, ROW_BLOCK), lambda b, be: (index(b), 0, 0),
                            memory_space=pltpu.SMEM)

    grid_spec = pltpu.PrefetchScalarGridSpec(
        num_scalar_prefetch=1,
        grid=(n_blocks,),
        in_specs=[
            view(lambda b: 0), view(lambda b: 1),
            view(lambda b: b + first - 1),
            view(lambda b: b + first),
            view(lambda b: first), view(lambda b: first + 1),
            view(lambda b: jnp.minimum(b + 2, n_blocks - 1) + first),
            pl.BlockSpec(memory_space=pl.ANY),
            pl.BlockSpec((1, 1, D, n_gu), lambda b, be: (layer, be[b], 0, 0)),
            pl.BlockSpec((1, 1, 1, n_gu), lambda b, be: (layer, be[b], 0, 0)),
            pl.BlockSpec((1, 1, de, D), lambda b, be: (layer, be[b], 0, 0)),
            pl.BlockSpec((1, 1, 1, D), lambda b, be: (layer, be[b], 0, 0)),
        ],
        out_specs=pl.BlockSpec(memory_space=pl.ANY),
        scratch_shapes=[pltpu.VMEM((D, n_gu), BF16), pltpu.VMEM((de, D), BF16)]
        + [pltpu.VMEM((ROW_BLOCK, D), F32)] * 6
        + [pltpu.SemaphoreType.DMA((3,)), pltpu.SemaphoreType.DMA((3,))],
    )
    return pl.pallas_call(
        functools.partial(_ffn_kernel, n_blocks=n_blocks),
        grid_spec=grid_spec,
        out_shape=jax.ShapeDtypeStruct((n_out, D), F32),
        compiler_params=pltpu.CompilerParams(
            dimension_semantics=("arbitrary",), vmem_limit_bytes=VMEM_LIMIT),
        name="moe_ffn",
    )(block_e, slots, slots, slots, slots, toks, toks, toks, h2,
      w_gu, b_gu.reshape(L, E, 1, n_gu), w_dn, b_dn.reshape(L, E, 1, D))


def _combine_kernel(*refs):
    yk_refs = refs[:TOP_K]
    meta_ref, x1_ref, post2_ref, gate2_ref, o_ref = refs[TOP_K:]
    y = meta_ref[:, META_G:META_G + 1] * yk_refs[0][...]
    for kk in range(1, TOP_K):
        y = y + meta_ref[:, META_G + kk:META_G + kk + 1] * yk_refs[kk][...]
    o_ref[...] = x1_ref[...] + gate2_ref[0] * (y * _rms(y) * post2_ref[...])


def _combine(yg, meta, x1, post2, gate2, *, seq, tm=512):
    T, D = x1.shape
    tps = seq // tm
    steps = T // tm
    return pl.pallas_call(
        _combine_kernel,
        grid=(steps,),
        in_specs=[pl.BlockSpec((tm, D), functools.partial(lambda i, k: (k * steps + i, 0), k=k))
                  for k in range(TOP_K)] + [
            pl.BlockSpec((tm, LANES), lambda i: (i, 0)),
            pl.BlockSpec((tm, D), lambda i: (i, 0)),
            pl.BlockSpec((1, D), lambda i: (0, 0)),
            pl.BlockSpec((1, 1, D), lambda i: (i // tps, 0, 0)),
        ],
        out_specs=pl.BlockSpec((tm, D), lambda i: (i, 0)),
        out_shape=jax.ShapeDtypeStruct((T, D), F32),
        compiler_params=pltpu.CompilerParams(
            dimension_semantics=("arbitrary",), vmem_limit_bytes=VMEM_LIMIT),
        name="moe_combine",
    )(*([yg] * TOP_K), meta, x1, post2, gate2)


def _block_diag(w):
    g, a, b = w.shape
    out = jnp.zeros((g * a, g * b), w.dtype)
    for i in range(g):
        out = out.at[i * a:(i + 1) * a, i * b:(i + 1) * b].set(w[i])
    return out


def kernel(x, c, ada_w, ada_b, pre1_g, post1_g, pre2_g, post2_g, w_in, pool_w, pool_b, pool_scale,
           w_br_attn, w_br_pool, w_out, router_w, router_b, w_gu, b_gu, w_dn, b_dn):
    B, S, D = x.shape
    L = ada_w.shape[0]
    T = B * S
    d_pool = D - SB_WIDTH
    n_rows = T * TOP_K + N_EXPERTS * ROW_BLOCK
    n_blocks = n_rows // ROW_BLOCK

    c_pad = jnp.pad(c, ((0, 8 - B), (0, 0)))
    mod = _ada(c_pad, ada_w, ada_b)[:, :B]

    x2d = x.reshape(T, D)
    for l in range(L):
        shift1, scale1, gate1, shift2, scale2, gate2 = [
            mod[l, :, i * D:(i + 1) * D].reshape(B, 1, D) for i in range(6)]

        qkv, u, g = _in_proj(x2d, pre1_g[l], scale1, shift1, w_in[l].astype(BF16), seq=S)
        attn = _attention(qkv.reshape(B, S, 3 * SB_WIDTH)).reshape(T, SB_WIDTH)

        rw_pad = jnp.pad(router_w[l], ((0, 0), (0, LANES - N_EXPERTS)))
        rw_hi = rw_pad.astype(BF16)
        rw_pad = jnp.stack([rw_hi, (rw_pad - rw_hi.astype(F32)).astype(BF16)])
        rb_pad = jnp.pad(router_b[l], (0, LANES - N_EXPERTS), constant_values=-jnp.inf)
        x1, h2, meta, route, counts = _mixer_tail(
            attn, u, g, x2d,
            _block_diag(pool_w[l]).astype(BF16), pool_b[l].reshape(1, d_pool),
            pool_scale[l].reshape(1, d_pool),
            w_br_attn[l].astype(BF16), w_br_pool[l].astype(BF16), w_out[l].astype(BF16),
            post1_g[l].reshape(1, D), gate1, pre2_g[l].reshape(1, D), scale2, shift2,
            rw_pad, rb_pad.reshape(1, LANES), seq=S)

        e_idx = route[META_E:META_E + TOP_K].astype(jnp.int32)
        rank = route[META_R:META_R + TOP_K].astype(jnp.int32)
        cnt = counts[0, :N_EXPERTS].astype(jnp.int32)
        padded = ((cnt + ROW_BLOCK - 1) // ROW_BLOCK) * ROW_BLOCK
        pend = jnp.cumsum(padded)
        pstart = pend - padded
        experts = jnp.arange(N_EXPERTS, dtype=jnp.int32)[:, None, None]
        first_row = jnp.sum(jnp.where(e_idx[None] == experts, pstart[:, None, None], 0), axis=0)
        dest = (first_row + rank).reshape(-1)
        blk0 = jnp.arange(n_blocks, dtype=jnp.int32) * ROW_BLOCK
        block_e = jnp.minimum(jnp.sum(pend[None, :] <= blk0[:, None], axis=1), N_EXPERTS - 1).astype(jnp.int32)
        gap = padded - cnt
        gap_end = jnp.cumsum(gap)
        p = jnp.arange(n_rows - T * TOP_K, dtype=jnp.int32)
        pe = jnp.sum(gap_end[None, :] <= p[:, None], axis=1)
        pe_c = jnp.minimum(pe, N_EXPERTS - 1)
        in_expert = pstart[pe_c] + cnt[pe_c] + (p - (gap_end - gap)[pe_c])
        pad_rows = jnp.where(pe < N_EXPERTS, in_expert, pend[-1] + (p - gap_end[-1])).astype(jnp.int32)

        slot = _row_map(dest, pad_rows)
        tok = jnp.where(slot < T * TOP_K, slot % T, 0)
        n_spare = N_SPARE_BLOCKS * ROW_BLOCK
        as_blocks = lambda v: v.reshape(n_blocks + N_SPARE_BLOCKS, 1, ROW_BLOCK)
        toks = as_blocks(jnp.concatenate([jnp.zeros((n_spare,), jnp.int32), tok]))
        slots = as_blocks(jnp.concatenate([n_rows + jnp.arange(n_spare, dtype=jnp.int32), slot]))
        yg = _ffn(block_e, toks, slots, h2, w_gu, b_gu, w_dn, b_dn, l)
        x2d = _combine(yg, meta, x1, post2_g[l].reshape(1, D), gate2, seq=S)
    return x2d.reshape(B, S, D)
```
